```python
import jax, jax.numpy as jnp
from jax import lax
import numpy as np

D_MODEL = 4096
BATCH = 4
SEQ = 2048
DEPTH = 2

HEAD_DIM = 128
N_RET_HEADS = D_MODEL // HEAD_DIM // 2
N_FOX_HEADS = D_MODEL // HEAD_DIM // 2
RET_WIDTH = N_RET_HEADS * HEAD_DIM
FOX_WIDTH = N_FOX_HEADS * HEAD_DIM
IN_SIZES = (RET_WIDTH, RET_WIDTH, RET_WIDTH, RET_WIDTH, FOX_WIDTH, FOX_WIDTH, FOX_WIDTH, N_FOX_HEADS)
IN_COLS = sum(IN_SIZES)
MIX_WIDTH = RET_WIDTH + FOX_WIDTH
RET_CHUNK = 128
Q_BLOCK = 128
ROPE_BASE = 10000.0
D_FF = 256 * ((8 * D_MODEL // 3 + 255) // 256)
N_EXPERTS = 8
TOP_K = 2
D_FF_EXPERT = 7 * D_MODEL // 4
POOL_WINDOWS = (2, 4, 8, 16)
POOL_GROUP = D_MODEL // len(POOL_WINDOWS)
N_EVEN = (DEPTH + 1) // 2
N_ODD = DEPTH // 2
EPS = 1e-6

kernel_name = "hybrid_retention_fox_pool_moe"

F32 = jnp.float32


def rmsnorm(x, g):
    xf = x.astype(F32)
    y = xf * lax.rsqrt(jnp.mean(xf * xf, axis=-1, keepdims=True) + EPS)
    return (y * g.astype(F32)).astype(x.dtype)


def head_layernorm(x):
    xf = x.astype(F32)
    mu = jnp.mean(xf, axis=-1, keepdims=True)
    xc = xf - mu
    var = jnp.mean(xc * xc, axis=-1, keepdims=True)
    return (xc * lax.rsqrt(var + EPS)).astype(x.dtype)


def to_heads(t, n_heads):
    B, S, _ = t.shape
    return t.reshape(B, S, n_heads, HEAD_DIM).transpose(0, 2, 1, 3)


def from_heads(t):
    B, H, S, d = t.shape
    return t.transpose(0, 2, 1, 3).reshape(B, S, H * d)


def rotary(x):
    S, d = x.shape[2], x.shape[3]
    half = d // 2
    inv = ROPE_BASE ** (-jnp.arange(half, dtype=F32) / half)
    ang = jnp.arange(S, dtype=F32)[:, None] * inv[None, :]
    cos, sin = jnp.cos(ang), jnp.sin(ang)
    x1, x2 = x[..., :half], x[..., half:]
    return jnp.concatenate([x1 * cos - x2 * sin, x2 * cos + x1 * sin], axis=-1).astype(x.dtype)


def retention_chunkwise(q, k, v):
    B, H, S, d = q.shape
    C = RET_CHUNK
    NC = S // C
    log_gamma = jnp.log1p(-(2.0 ** (-5.0 - jnp.arange(H, dtype=F32))))
    j = jnp.arange(C, dtype=F32)
    diff = j[:, None] - j[None, :]
    decay_in = jnp.where(diff >= 0, jnp.exp(log_gamma[:, None, None] * jnp.maximum(diff, 0.0)), 0.0)
    k = k * (HEAD_DIM ** -0.5)
    qc = q.reshape(B, H, NC, C, d)
    kc = k.reshape(B, H, NC, C, d)
    vc = v.reshape(B, H, NC, C, d)
    scores = jnp.einsum('bhncd,bhnmd->bhncm', qc, kc) * decay_in[None, :, None]
    inner = jnp.einsum('bhncm,bhnme->bhnce', scores, vc)
    zeta = jnp.exp(log_gamma[:, None] * (C - 1 - j)[None, :])
    kv = jnp.einsum('bhnmd,bhnme->bhnde', kc * zeta[None, :, None, :, None], vc)
    gamma_c = jnp.exp(log_gamma * C)[None, :, None, None]

    def step(state, kv_n):
        return gamma_c * state + kv_n, state

    _, states = lax.scan(step, jnp.zeros_like(kv[:, :, 0]), jnp.moveaxis(kv, 2, 0))
    states = jnp.moveaxis(states, 0, 2)
    q_decay = jnp.exp(log_gamma[:, None] * (j + 1.0)[None, :])
    cross = jnp.einsum('bhncd,bhnde->bhnce', qc * q_decay[None, :, None, :, None], states)
    return (inner + cross).reshape(B, H, S, d).astype(q.dtype)


def forgetting_attention(q, k, v, f_logit):
    S = q.shape[2]
    scale = HEAD_DIM ** -0.5
    c = jnp.cumsum(jax.nn.log_sigmoid(f_logit.astype(F32)), axis=-1)
    outs = []
    for blk in range(S // Q_BLOCK):
        q0, q1 = blk * Q_BLOCK, (blk + 1) * Q_BLOCK
        s = jnp.einsum('bhqd,bhkd->bhqk', q[:, :, q0:q1], k[:, :, :q1]).astype(F32) * scale
        s = s + c[:, :, q0:q1, None] - c[:, :, None, :q1]
        mask = (q0 + jnp.arange(Q_BLOCK))[:, None] >= jnp.arange(q1)[None, :]
        p = jax.nn.softmax(jnp.where(mask, s, -jnp.inf), axis=-1)
        outs.append(jnp.einsum('bhqk,bhkd->bhqd', p.astype(v.dtype), v[:, :, :q1]))
    return jnp.concatenate(outs, axis=2)


def parallel_ret_fox_mixer(h, w_in, b_forget, w_out):
    proj = h @ w_in
    parts = []
    off = 0
    for size in IN_SIZES:
        parts.append(proj[..., off:off + size])
        off += size
    rq, rk, rv, rg, fq, fk, fv, ff = parts
    ret = retention_chunkwise(rotary(to_heads(rq, N_RET_HEADS)), rotary(to_heads(rk, N_RET_HEADS)),
                              to_heads(rv, N_RET_HEADS))
    ret = jax.nn.silu(rg) * from_heads(head_layernorm(ret)).astype(rg.dtype)
    f_logit = (ff + b_forget).transpose(0, 2, 1)
    fox = from_heads(forgetting_attention(to_heads(fq, N_FOX_HEADS), to_heads(fk, N_FOX_HEADS),
                                          to_heads(fv, N_FOX_HEADS), f_logit))
    return (jnp.concatenate([ret, fox.astype(ret.dtype)], axis=-1) @ w_out).astype(h.dtype)


def multiscale_pool_mixer(h, w_pool, pool_scale):
    B, S, D = h.shape
    hf = h.astype(F32)
    cs = jnp.concatenate([jnp.zeros((B, 1, D), F32), jnp.cumsum(hf, axis=1)], axis=1)
    outs = []
    for g, w in enumerate(POOL_WINDOWS):
        lo, hi = g * POOL_GROUP, (g + 1) * POOL_GROUP
        csg = cs[:, :, lo:hi]
        upper = csg[:, 1:]
        lower = jnp.concatenate([jnp.zeros((B, w - 1, POOL_GROUP), F32), csg[:, :S - w + 1]], axis=1)
        count = jnp.minimum(jnp.arange(1, S + 1), w).astype(F32)[None, :, None]
        pooled = (upper - lower) / count - hf[:, :, lo:hi]
        outs.append(jnp.einsum('bsc,ce->bse', pooled.astype(h.dtype), w_pool[g]))
    return (jnp.concatenate(outs, axis=-1) * pool_scale).astype(h.dtype)


def swiglu(h, w_gate, w_up, w_down):
    return (jax.nn.silu(h @ w_gate) * (h @ w_up)) @ w_down


def moe_swiglu(h, w_router, we_gate, we_up, we_down):
    B, S, D = h.shape
    t = h.reshape(B * S, D)
    logits = (t @ w_router).astype(F32)
    top_vals, top_idx = lax.top_k(logits, TOP_K)
    top_w = jax.nn.softmax(top_vals, axis=-1)
    gates = jnp.sum(jax.nn.one_hot(top_idx, N_EXPERTS, dtype=F32) * top_w[..., None], axis=1)
    y = jnp.zeros_like(t)
    for e in range(N_EXPERTS):
        y = y + gates[:, e:e + 1].astype(t.dtype) * swiglu(t, we_gate[e], we_up[e], we_down[e])
    return y.reshape(B, S, D)


def setup_inputs(seed: int = 0) -> dict:
    key = jax.random.key(seed)
    ks = jax.random.split(key, 32)

    def w(k, shape, fan_in):
        return jax.random.normal(k, shape, F32) * (fan_in ** -0.5)

    def gain(k, n):
        return 1.0 + 0.1 * jax.random.normal(k, (n, D_MODEL), F32)

    return {
        "x": jax.random.normal(ks[0], (BATCH, SEQ, D_MODEL), F32),
        "even_norm_mix_pre": gain(ks[1], N_EVEN),
        "even_w_in": w(ks[2], (N_EVEN, D_MODEL, IN_COLS), D_MODEL),
        "even_b_forget": 0.1 * jax.random.normal(ks[3], (N_EVEN, N_FOX_HEADS), F32),
        "even_w_out": w(ks[4], (N_EVEN, MIX_WIDTH, D_MODEL), MIX_WIDTH),
        "even_norm_mix_post": gain(ks[5], N_EVEN),
        "even_norm_ffn_pre": gain(ks[6], N_EVEN),
        "even_w_gate": w(ks[7], (N_EVEN, D_MODEL, D_FF), D_MODEL),
        "even_w_up": w(ks[8], (N_EVEN, D_MODEL, D_FF), D_MODEL),
        "even_w_down": w(ks[9], (N_EVEN, D_FF, D_MODEL), D_FF),
        "even_norm_ffn_post": gain(ks[10], N_EVEN),
        "odd_norm_mix_pre": gain(ks[11], N_ODD),
        "odd_w_pool": w(ks[12], (N_ODD, len(POOL_WINDOWS), POOL_GROUP, POOL_GROUP), POOL_GROUP),
        "odd_pool_scale": gain(ks[13], N_ODD),
        "odd_norm_mix_post": gain(ks[14], N_ODD),
        "odd_norm_ffn_pre": gain(ks[15], N_ODD),
        "odd_w_router": w(ks[16], (N_ODD, D_MODEL, N_EXPERTS), D_MODEL),
        "odd_we_gate": w(ks[17], (N_ODD, N_EXPERTS, D_MODEL, D_FF_EXPERT), D_MODEL),
        "odd_we_up": w(ks[18], (N_ODD, N_EXPERTS, D_MODEL, D_FF_EXPERT), D_MODEL),
        "odd_we_down": w(ks[19], (N_ODD, N_EXPERTS, D_FF_EXPERT, D_MODEL), D_FF_EXPERT),
        "odd_norm_ffn_post": gain(ks[20], N_ODD),
    }


def reference(x, even_norm_mix_pre, even_w_in, even_b_forget, even_w_out, even_norm_mix_post,
              even_norm_ffn_pre, even_w_gate, even_w_up, even_w_down, even_norm_ffn_post,
              odd_norm_mix_pre, odd_w_pool, odd_pool_scale, odd_norm_mix_post, odd_norm_ffn_pre,
              odd_w_router, odd_we_gate, odd_we_up, odd_we_down, odd_norm_ffn_post):
    for layer in range(DEPTH):
        i = layer // 2
        if layer % 2 == 0:
            m = parallel_ret_fox_mixer(rmsnorm(x, even_norm_mix_pre[i]), even_w_in[i], even_b_forget[i], even_w_out[i])
            x = x + rmsnorm(m, even_norm_mix_post[i])
            f = swiglu(rmsnorm(x, even_norm_ffn_pre[i]), even_w_gate[i], even_w_up[i], even_w_down[i])
            x = x + rmsnorm(f, even_norm_ffn_post[i])
        else:
            m = multiscale_pool_mixer(rmsnorm(x, odd_norm_mix_pre[i]), odd_w_pool[i], odd_pool_scale[i])
            x = x + rmsnorm(m, odd_norm_mix_post[i])
            f = moe_swiglu(rmsnorm(x, odd_norm_ffn_pre[i]), odd_w_router[i], odd_we_gate[i], odd_we_up[i], odd_we_down[i])
            x = x + rmsnorm(f, odd_norm_ffn_post[i])
    return x
```

```python
import functools

import jax
import jax.numpy as jnp
from jax import lax
from jax.experimental import pallas as pl
from jax.experimental.pallas import tpu as pltpu

F32 = jnp.float32
BF16 = jnp.bfloat16
U32 = jnp.uint32
I32 = jnp.int32

HEAD_DIM = 128
N_RET_HEADS = 16
N_FOX_HEADS = 16
RET_CHUNK = 128
ROPE_BASE = 10000.0
N_EXPERTS = 8
POOL_WINDOWS = (2, 4, 8, 16)
EPS = 1e-6
LANES = 128
NEG_BIG = -1e30

VMEM_LIMIT = 56 * 1024 * 1024
CAST_ROWS = 256
MOE_TILE = 512


def _cparams(n_axes):
    return pltpu.CompilerParams(
        dimension_semantics=("arbitrary",) * n_axes, vmem_limit_bytes=VMEM_LIMIT)


def _rms(v):
    return lax.rsqrt(jnp.mean(v * v, axis=-1, keepdims=True) + EPS)


def _sigmoid(v):
    return 1.0 / (1.0 + jnp.exp(-v))


def _rms_cast_kernel(x_ref, g_ref, o_ref):
    x = x_ref[...]
    o_ref[...] = ((x * _rms(x)) * g_ref[...]).astype(o_ref.dtype)


def rms_cast(x, g, tm=256):
    n, d = x.shape
    return pl.pallas_call(
        _rms_cast_kernel,
        grid=(n // tm,),
        in_specs=[pl.BlockSpec((tm, d), lambda i: (i, 0)),
                  pl.BlockSpec((1, d), lambda i: (0, 0))],
        out_specs=pl.BlockSpec((tm, d), lambda i: (i, 0)),
        out_shape=jax.ShapeDtypeStruct((n, d), BF16),
        compiler_params=_cparams(1),
        name="rms_cast",
    )(x, g.reshape(1, d))


def _pack_bf16_pairs(h):
    half = h.shape[1] // 2
    bits = lax.bitcast_convert_type(h.astype(BF16).astype(F32), U32)
    return (bits[:, :half] >> 16) | (bits[:, half:] & jnp.uint32(0xFFFF0000))


def _unpack_bf16_pairs(p):
    lo = lax.bitcast_convert_type(p << 16, F32).astype(BF16)
    hi = lax.bitcast_convert_type(p & jnp.uint32(0xFFFF0000), F32).astype(BF16)
    return lo, hi


def _resid_norm_kernel(*refs, h_mode, router):
    x_ref, m_ref, gp_ref = refs[:3]
    pos = 3
    gn_ref = wr_ref = None
    if h_mode is not None:
        gn_ref = refs[pos]; pos += 1
    if router:
        wr_ref = refs[pos]; pos += 1
    xo_ref = refs[pos]; pos += 1
    m = m_ref[...]
    xn = x_ref[...] + (m * _rms(m)) * gp_ref[...]
    xo_ref[...] = xn
    if h_mode is None:
        return
    h = (xn * _rms(xn)) * gn_ref[...]
    h_ref = refs[pos]; pos += 1
    if h_mode == "packed":
        h_ref[...] = _pack_bf16_pairs(h)
    else:
        h_ref[...] = h.astype(h_ref.dtype)
    if router:
        refs[pos][...] = jnp.dot(h, wr_ref[...], preferred_element_type=F32,
                                 precision=lax.Precision.HIGHEST)


def resid_norm(x, m, g_post, g_next=None, h_mode=None, w_router=None, tm=256):
    n, d = x.shape
    row = pl.BlockSpec((tm, d), lambda i: (i, 0))
    vec = pl.BlockSpec((1, d), lambda i: (0, 0))
    args = [x, m, g_post.reshape(1, d)]
    in_specs = [row, row, vec]
    out_shape = [jax.ShapeDtypeStruct((n, d), F32)]
    out_specs = [row]
    if h_mode is not None:
        args.append(g_next.reshape(1, d)); in_specs.append(vec)
        if h_mode == "packed":
            out_shape.append(jax.ShapeDtypeStruct((n, d // 2), U32))
            out_specs.append(pl.BlockSpec((tm, d // 2), lambda i: (i, 0)))
        else:
            out_shape.append(jax.ShapeDtypeStruct((n, d), BF16 if h_mode == "bf16" else F32))
            out_specs.append(row)
    router = w_router is not None
    if router:
        args.append(w_router)
        in_specs.append(pl.BlockSpec((d, LANES), lambda i: (0, 0)))
        out_shape.append(jax.ShapeDtypeStruct((n, LANES), F32))
        out_specs.append(pl.BlockSpec((tm, LANES), lambda i: (i, 0)))
    return pl.pallas_call(
        functools.partial(_resid_norm_kernel, h_mode=h_mode, router=router),
        grid=(n // tm,),
        in_specs=in_specs, out_specs=out_specs, out_shape=out_shape,
        compiler_params=_cparams(1),
        name="resid_norm",
    )(*args)


def _gmm_kernel(te_ref, first_ref, valid_ref, rowblk_ref, *refs, n_a, n_w, has_scale):
    del te_ref, rowblk_ref
    a_refs = refs[:n_a]
    w_refs = refs[n_a:n_a + n_w]
    pos = n_a + n_w
    s_ref = None
    if has_scale:
        s_ref = refs[pos]; pos += 1
    o_ref = refs[pos]; pos += 1
    wbf_refs = refs[pos:pos + n_w]
    i = pl.program_id(1)
    k = w_refs[0].shape[1]

    @pl.when(first_ref[i] == 1)
    def _():
        def cast(c, carry):
            r = pl.multiple_of(c * CAST_ROWS, CAST_ROWS)
            for w_ref, wbf in zip(w_refs, wbf_refs):
                wbf[pl.ds(r, CAST_ROWS), :] = w_ref[0, pl.ds(r, CAST_ROWS), :].astype(BF16)
            return carry
        lax.fori_loop(0, k // CAST_ROWS, cast, 0)

    def matmul(wbf):
        acc = None
        off = 0
        for a_ref in a_refs:
            ka = a_ref.shape[1]
            part = jnp.dot(a_ref[...], wbf[off:off + ka, :], preferred_element_type=F32)
            acc = part if acc is None else acc + part
            off += ka
        return acc

    @pl.when(valid_ref[i] == 1)
    def _():
        acc = matmul(wbf_refs[0])
        if n_w == 2:
            acc = (acc * _sigmoid(acc)) * matmul(wbf_refs[1])
        if has_scale:
            acc = acc * s_ref[...]
        o_ref[...] = acc.astype(o_ref.dtype)

    @pl.when(valid_ref[i] == 0)
    def _():
        o_ref[...] = jnp.zeros_like(o_ref)


def _dense_meta(n_tiles):
    first = jnp.zeros((n_tiles,), I32).at[0].set(1)
    return (jnp.zeros((n_tiles,), I32), first, jnp.ones((n_tiles,), I32),
            jnp.arange(n_tiles, dtype=I32))


def gmm(a, ws, *, tm, tn, n_out, out_dtype, meta=None, k=None, scale=None,
        a_col=None, w_group=None, w_col=None):
    a_parts = list(a) if isinstance(a, (list, tuple)) else [a]
    rows = a_parts[0].shape[0]
    if k is None:
        k = sum(p.shape[1] for p in a_parts)
    n_tiles = rows // tm
    if meta is None:
        meta = _dense_meta(n_tiles)
    a_col = a_col or (lambda j: 0)
    w_col = w_col or (lambda j: j)
    if w_group is None:
        w_map = lambda j, i, te, fi, va, rb: (te[i], 0, w_col(j))
    else:
        w_map = lambda j, i, te, fi, va, rb: (w_group(j), 0, w_col(j))
    a_map = lambda j, i, te, fi, va, rb: (rb[i], a_col(j))
    in_specs = [pl.BlockSpec((tm, k if len(a_parts) == 1 else p.shape[1]), a_map)
                for p in a_parts]
    in_specs += [pl.BlockSpec((1, k, tn), w_map) for _ in ws]
    args = [*a_parts, *ws]
    if scale is not None:
        in_specs.append(pl.BlockSpec((1, tn), lambda j, i, te, fi, va, rb: (0, j)))
        args.append(scale)
    grid_spec = pltpu.PrefetchScalarGridSpec(
        num_scalar_prefetch=4,
        grid=(n_out // tn, n_tiles),
        in_specs=in_specs,
        out_specs=pl.BlockSpec((tm, tn), lambda j, i, te, fi, va, rb: (i, j)),
        scratch_shapes=[pltpu.VMEM((k, tn), BF16) for _ in ws],
    )
    return pl.pallas_call(
        functools.partial(_gmm_kernel, n_a=len(a_parts), n_w=len(ws),
                          has_scale=scale is not None),
        grid_spec=grid_spec,
        out_shape=jax.ShapeDtypeStruct((rows, n_out), out_dtype),
        compiler_params=_cparams(2),
        name="gmm",
    )(*meta, *args)


def _forget_cumsum_kernel(f_ref, b_ref, o_ref):
    x = f_ref[...] + b_ref[...]
    ls = jnp.minimum(x, 0.0) - jnp.log1p(jnp.exp(-jnp.abs(x)))
    s = ls.shape[0]
    row = lax.broadcasted_iota(I32, ls.shape, 0)
    sh = 1
    while sh < s:
        ls = ls + jnp.where(row >= sh, pltpu.roll(ls, sh, 0), 0.0)
        sh *= 2
    o_ref[...] = ls


def forget_cumsum(f, b, batch):
    n, w = f.shape
    s = n // batch
    return pl.pallas_call(
        _forget_cumsum_kernel,
        grid=(batch,),
        in_specs=[pl.BlockSpec((s, w), lambda i: (i, 0)),
                  pl.BlockSpec((1, w), lambda i: (0, 0))],
        out_specs=pl.BlockSpec((s, w), lambda i: (i, 0)),
        out_shape=jax.ShapeDtypeStruct((n, w), F32),
        compiler_params=_cparams(1),
        name="forget_cumsum",
    )(f, b)


def _retention_kernel(lg_ref, q_ref, k_ref, v_ref, g_ref, cos_ref, sin_ref, o_ref):
    c = RET_CHUNK
    d = HEAD_DIM
    s = q_ref.shape[0]
    lg = lg_ref[pl.program_id(1)]
    ii = lax.broadcasted_iota(I32, (c, c), 0)
    jj = lax.broadcasted_iota(I32, (c, c), 1)
    diff = (ii - jj).astype(F32)
    decay_in = jnp.where(diff >= 0, jnp.exp(lg * jnp.maximum(diff, 0.0)), 0.0)
    jc = lax.broadcasted_iota(I32, (c, 1), 0).astype(F32)
    zeta = jnp.exp(lg * (c - 1 - jc))
    q_decay = jnp.exp(lg * (jc + 1.0))
    gamma_c = jnp.exp(jnp.full((1, 1), lg, F32) * c)
    scale = d ** -0.5
    nt = (((1,), (1,)), ((), ()))
    tn = (((0,), (0,)), ((), ()))
    state = jnp.zeros((d, d), F32)
    for n in range(s // c):
        sl = pl.ds(n * c, c)
        cos = cos_ref[sl, :]
        sin = sin_ref[sl, :]
        q = q_ref[sl, :].astype(F32)
        k = k_ref[sl, :].astype(F32)
        v = v_ref[sl, :]
        qr = q * cos + pltpu.roll(q, d // 2, 1) * sin
        kr = (k * cos + pltpu.roll(k, d // 2, 1) * sin) * scale
        scores = lax.dot_general(qr.astype(BF16), kr.astype(BF16), nt,
                                 preferred_element_type=F32) * decay_in
        inner = jnp.dot(scores.astype(BF16), v, preferred_element_type=F32)
        cross = jnp.dot((qr * q_decay).astype(BF16), state.astype(BF16),
                        preferred_element_type=F32)
        kv = lax.dot_general((kr * zeta).astype(BF16), v, tn, preferred_element_type=F32)
        state = gamma_c * state + kv
        o = inner + cross
        xc = o - jnp.mean(o, axis=-1, keepdims=True)
        y = xc * lax.rsqrt(jnp.mean(xc * xc, axis=-1, keepdims=True) + EPS)
        g = g_ref[sl, :].astype(F32)
        o_ref[sl, :] = ((g * _sigmoid(g)) * y).astype(o_ref.dtype)


def retention(proj, cos, sin, log_gamma, batch):
    n = proj.shape[0]
    s = n // batch
    d = HEAD_DIM
    hh = N_RET_HEADS

    def col(off):
        return pl.BlockSpec((s, d), lambda b, h, lg: (b, off + h))

    tab = pl.BlockSpec((s, d), lambda b, h, lg: (0, 0))
    grid_spec = pltpu.PrefetchScalarGridSpec(
        num_scalar_prefetch=1,
        grid=(batch, hh),
        in_specs=[col(0), col(hh), col(2 * hh), col(3 * hh), tab, tab],
        out_specs=pl.BlockSpec((s, d), lambda b, h, lg: (b, h)),
    )
    return pl.pallas_call(
        _retention_kernel,
        grid_spec=grid_spec,
        out_shape=jax.ShapeDtypeStruct((n, hh * d), BF16),
        compiler_params=_cparams(2),
        name="retention",
    )(log_gamma, proj, proj, proj, proj, cos, sin)


def _fox_kernel(q_ref, k_ref, v_ref, ccol_ref, crow_ref, o_ref, *, tq):
    s, d = q_ref.shape
    h = pl.program_id(1)
    scale = d ** -0.5
    nt = (((1,), (1,)), ((), ()))
    lane = lax.broadcasted_iota(I32, (tq, LANES), 1)
    rr = lax.broadcasted_iota(I32, (tq, tq), 0)
    cc = lax.broadcasted_iota(I32, (tq, tq), 1)

    for qi in range(s // tq):
        qs = pl.ds(qi * tq, tq)
        q = q_ref[qs, :]
        cq = jnp.sum(jnp.where(lane == h, ccol_ref[qs, :], 0.0), axis=-1, keepdims=True)

        def scores(kb):
            ks = pl.ds(pl.multiple_of(kb * tq, tq), tq)
            sc = lax.dot_general(q, k_ref[ks, :], nt, preferred_element_type=F32) * scale
            return sc + cq - crow_ref[:, ks], v_ref[ks, :]

        def update(carry, sc, v):
            m, l, acc = carry
            m_new = jnp.maximum(m, jnp.max(sc, axis=-1, keepdims=True))
            alpha = jnp.exp(m - m_new)
            p = jnp.exp(sc - m_new)
            l = alpha * l + jnp.sum(p, axis=-1, keepdims=True)
            acc = alpha * acc + jnp.dot(p.astype(BF16), v, preferred_element_type=F32)
            return m_new, l, acc

        def body(kb, carry):
            sc, v = scores(kb)
            return update(carry, sc, v)

        carry = (jnp.full((tq, 1), NEG_BIG, F32), jnp.zeros((tq, 1), F32),
                 jnp.zeros((tq, d), F32))
        carry = lax.fori_loop(0, qi, body, carry)
        sc, v = scores(qi)
        _, l, acc = update(carry, jnp.where(rr >= cc, sc, NEG_BIG), v)
        o_ref[qs, :] = (acc / l).astype(o_ref.dtype)


def fox_attention(proj, c_cols, c_rows, batch, tq=256):
    n = proj.shape[0]
    s = n // batch
    d = HEAD_DIM
    base = 4 * N_RET_HEADS
    hh = N_FOX_HEADS

    def col(off):
        return pl.BlockSpec((s, d), lambda b, h: (b, off + h))

    return pl.pallas_call(
        functools.partial(_fox_kernel, tq=tq),
        grid=(batch, hh),
        in_specs=[col(base), col(base + hh), col(base + 2 * hh),
                  pl.BlockSpec((s, LANES), lambda b, h: (b, 0)),
                  pl.BlockSpec((None, None, 1, s), lambda b, h: (b, h, 0, 0))],
        out_specs=pl.BlockSpec((s, d), lambda b, h: (b, h)),
        out_shape=jax.ShapeDtypeStruct((n, hh * d), BF16),
        compiler_params=_cparams(2),
        name="fox_attention",
    )(proj, proj, proj, c_cols, c_rows)


def _pool_kernel(h_ref, o_ref, *, blocks_per_group):
    x = h_ref[...]
    row = lax.broadcasted_iota(I32, x.shape, 0)
    group = pl.program_id(1) // blocks_per_group

    def shifted(a, by):
        return jnp.where(row >= by, pltpu.roll(a, by, 0), 0.0)

    for gi, w in enumerate(POOL_WINDOWS):
        @pl.when(group == gi)
        def _():
            acc = x
            by = 1
            while by < w:
                acc = acc + shifted(acc, by)
                by *= 2
            count = jnp.minimum(row + 1, w).astype(F32)
            o_ref[...] = (acc / count - x).astype(o_ref.dtype)


def pool(h, batch, tc=256):
    n, d = h.shape
    s = n // batch
    group = d // len(POOL_WINDOWS)
    return pl.pallas_call(
        functools.partial(_pool_kernel, blocks_per_group=group // tc),
        grid=(batch, d // tc),
        in_specs=[pl.BlockSpec((s, tc), lambda b, j: (b, j))],
        out_specs=pl.BlockSpec((s, tc), lambda b, j: (b, j)),
        out_shape=jax.ShapeDtypeStruct((n, d), BF16),
        compiler_params=_cparams(2),
        name="pool",
    )(h)


def _route_kernel(lg_ref, ids_ref, gate_ref, cnt_ref, carry_ref):
    tt = lg_ref.shape[0]

    @pl.when(pl.program_id(0) == 0)
    def _():
        carry_ref[...] = jnp.zeros_like(carry_ref)

    lane = lax.broadcasted_iota(I32, (tt, LANES), 1)
    lane_f = lane.astype(F32)
    x = jnp.where(lane < N_EXPERTS, lg_ref[...], -jnp.inf)
    v1 = jnp.max(x, axis=-1, keepdims=True)
    i1 = jnp.min(jnp.where(x == v1, lane_f, float(LANES)), axis=-1, keepdims=True)
    x2 = jnp.where(lane_f == i1, -jnp.inf, x)
    v2 = jnp.max(x2, axis=-1, keepdims=True)
    i2 = jnp.min(jnp.where(x2 == v2, lane_f, float(LANES)), axis=-1, keepdims=True)
    e = jnp.exp(v2 - v1)
    w1 = 1.0 / (1.0 + e)
    w2 = e / (1.0 + e)
    sel1 = lane_f == i1
    sel2 = lane_f == i2
    onehot = jnp.where(sel1, 1.0, jnp.where(sel2, 1.0, 0.0))
    rr = lax.broadcasted_iota(I32, (tt, tt), 0)
    cc = lax.broadcasted_iota(I32, (tt, tt), 1)
    tri = jnp.where(rr > cc, 1.0, 0.0).astype(BF16)
    before = jnp.dot(tri, onehot.astype(BF16), preferred_element_type=F32) + carry_ref[0:1, :]
    r1 = jnp.sum(jnp.where(sel1, before, 0.0), axis=-1, keepdims=True)
    r2 = jnp.sum(jnp.where(sel2, before, 0.0), axis=-1, keepdims=True)
    ids = jnp.where(lane == 0, i1, jnp.where(lane == 1, i2,
                    jnp.where(lane == 2, r1, jnp.where(lane == 3, r2, 0.0))))
    ids_ref[...] = ids.astype(I32)
    gate_ref[...] = jnp.where(lane == 0, w1, jnp.where(lane == 1, w2, 0.0))
    total = carry_ref[0:1, :] + jnp.sum(onehot, axis=0, keepdims=True)
    carry_ref[...] = jnp.broadcast_to(total, carry_ref.shape)
    cnt_ref[...] = jnp.broadcast_to(total, cnt_ref.shape).astype(I32)


def route(logits, tt=256):
    n = logits.shape[0]
    row = pl.BlockSpec((tt, LANES), lambda i: (i, 0))
    return pl.pallas_call(
        _route_kernel,
        grid=(n // tt,),
        in_specs=[row],
        out_specs=[row, row, pl.BlockSpec((8, LANES), lambda i: (0, 0))],
        out_shape=[jax.ShapeDtypeStruct((n, LANES), I32),
                   jax.ShapeDtypeStruct((n, LANES), F32),
                   jax.ShapeDtypeStruct((8, LANES), I32)],
        scratch_shapes=[pltpu.VMEM((8, LANES), F32)],
        compiler_params=_cparams(1),
        name="route",
    )(logits)


def _gather_kernel(pos1_ref, pos2_ref, h_hbm, o_ref, src_ref, buf_ref, sem, *, n_tokens):
    tg, half = buf_ref.shape
    i = pl.program_id(0)

    @pl.when(i == 0)
    def _():
        def clear(p, carry):
            src_ref[p] = 0
            return carry
        lax.fori_loop(0, src_ref.shape[0], clear, 0)

        def place(t, carry):
            src_ref[pos1_ref[t]] = t
            src_ref[pos2_ref[t]] = t
            return carry
        lax.fori_loop(0, n_tokens, place, 0)

    def row_copy(r):
        return pltpu.make_async_copy(
            h_hbm.at[pl.ds(src_ref[i * tg + r], 1), :], buf_ref.at[pl.ds(r, 1), :], sem)

    def issue(r, carry):
        row_copy(r).start()
        return carry
    lax.fori_loop(0, tg, issue, 0)

    def drain(r, carry):
        row_copy(r).wait()
        return carry
    lax.fori_loop(0, tg, drain, 0)

    lo, hi = _unpack_bf16_pairs(buf_ref[...])
    o_ref[:, :half] = lo
    o_ref[:, half:] = hi


def gather_rows(h_packed, pos1, pos2, n_rows, tg=256):
    n, half = h_packed.shape
    grid_spec = pltpu.PrefetchScalarGridSpec(
        num_scalar_prefetch=2,
        grid=(n_rows // tg,),
        in_specs=[pl.BlockSpec(memory_space=pl.ANY)],
        out_specs=pl.BlockSpec((tg, 2 * half), lambda i, p1, p2: (i, 0)),
        scratch_shapes=[pltpu.SMEM((n_rows,), I32), pltpu.VMEM((tg, half), U32),
                        pltpu.SemaphoreType.DMA(())],
    )
    return pl.pallas_call(
        functools.partial(_gather_kernel, n_tokens=n),
        grid_spec=grid_spec,
        out_shape=jax.ShapeDtypeStruct((n_rows, 2 * half), BF16),
        compiler_params=_cparams(1),
        name="gather_rows",
    )(pos1, pos2, h_packed)


def _combine_kernel(pos1_ref, pos2_ref, o_hbm, gate_ref, x_ref, g_ref, out_ref, buf_ref, sem):
    tt = x_ref.shape[0]
    i = pl.program_id(0)

    def row_copy(r, slot, pos_ref):
        return pltpu.make_async_copy(
            o_hbm.at[pl.ds(pos_ref[i * tt + r], 1), :], buf_ref.at[slot, pl.ds(r, 1), :], sem)

    def issue(r, carry):
        row_copy(r, 0, pos1_ref).start()
        row_copy(r, 1, pos2_ref).start()
        return carry
    lax.fori_loop(0, tt, issue, 0)

    def drain(r, carry):
        row_copy(r, 0, pos1_ref).wait()
        row_copy(r, 1, pos2_ref).wait()
        return carry
    lax.fori_loop(0, tt, drain, 0)

    gate = gate_ref[...]
    y = gate[:, 0:1] * buf_ref[0] + gate[:, 1:2] * buf_ref[1]
    out_ref[...] = x_ref[...] + (y * _rms(y)) * g_ref[...]


def combine(o, pos1, pos2, gate, x, g, tt=128):
    n, d = x.shape
    row = lambda i, p1, p2: (i, 0)
    grid_spec = pltpu.PrefetchScalarGridSpec(
        num_scalar_prefetch=2,
        grid=(n // tt,),
        in_specs=[pl.BlockSpec(memory_space=pl.ANY),
                  pl.BlockSpec((tt, LANES), row),
                  pl.BlockSpec((tt, d), row),
                  pl.BlockSpec((1, d), lambda i, p1, p2: (0, 0))],
        out_specs=pl.BlockSpec((tt, d), row),
        scratch_shapes=[pltpu.VMEM((2, tt, d), F32), pltpu.SemaphoreType.DMA(())],
    )
    return pl.pallas_call(
        _combine_kernel,
        grid_spec=grid_spec,
        out_shape=jax.ShapeDtypeStruct((n, d), F32),
        compiler_params=_cparams(1),
        name="combine",
    )(pos1, pos2, o, gate, x, g.reshape(1, d))


def _rope_tables(s):
    half = HEAD_DIM // 2
    inv = ROPE_BASE ** (-jnp.arange(half, dtype=F32) / half)
    ang = jnp.arange(s, dtype=F32)[:, None] * inv[None, :]
    cos, sin = jnp.cos(ang), jnp.sin(ang)
    return jnp.concatenate([cos, cos], axis=-1), jnp.concatenate([-sin, sin], axis=-1)


def _moe_layout(ids, counts, tm, n_tiles):
    counts = counts[0, :N_EXPERTS]
    tiles_per = (counts + tm - 1) // tm
    tile_end = jnp.cumsum(tiles_per)
    tile_start = tile_end - tiles_per
    row_start = tile_start * tm
    pos1 = row_start[ids[:, 0]] + ids[:, 2]
    pos2 = row_start[ids[:, 1]] + ids[:, 3]
    t = jnp.arange(n_tiles, dtype=I32)
    n_used = tile_end[-1]
    te = jnp.minimum(jnp.searchsorted(tile_end, t, side="right"), N_EXPERTS - 1).astype(I32)
    last = jnp.maximum(n_used - 1, 0)
    te = jnp.where(t < n_used, te, te[last])
    valid = (t < n_used).astype(I32)
    prev = jnp.concatenate([jnp.full((1,), -1, I32), te[:-1]])
    first = ((te != prev) | (t == 0)).astype(I32)
    rowblk = jnp.minimum(t, last).astype(I32)
    return pos1.astype(I32), pos2.astype(I32), (te, first, valid, rowblk)


def kernel(x, even_norm_mix_pre, even_w_in, even_b_forget, even_w_out, even_norm_mix_post,
           even_norm_ffn_pre, even_w_gate, even_w_up, even_w_down, even_norm_ffn_post,
           odd_norm_mix_pre, odd_w_pool, odd_pool_scale, odd_norm_mix_post, odd_norm_ffn_pre,
           odd_w_router, odd_we_gate, odd_we_up, odd_we_down, odd_norm_ffn_post):
    batch, seq, d = x.shape
    n = batch * seq
    x0 = x.reshape(n, d)
    ret_w = N_RET_HEADS * HEAD_DIM
    fox_w = N_FOX_HEADS * HEAD_DIM
    main_cols = 4 * ret_w + 3 * fox_w

    h0 = rms_cast(x0, even_norm_mix_pre[0])
    proj = gmm(h0, [even_w_in], tm=512, tn=512, n_out=main_cols, out_dtype=BF16)
    w_forget = jnp.pad(even_w_in[:, :, main_cols:], ((0, 0), (0, 0), (0, LANES - N_FOX_HEADS)))
    b_forget = jnp.pad(even_b_forget[0], (0, LANES - N_FOX_HEADS)).reshape(1, LANES)
    f_logit = gmm(h0, [w_forget], tm=512, tn=LANES, n_out=LANES, out_dtype=F32)
    c_cols = forget_cumsum(f_logit, b_forget, batch)
    c_rows = c_cols[:, :N_FOX_HEADS].reshape(batch, seq, N_FOX_HEADS)
    c_rows = c_rows.transpose(0, 2, 1).reshape(batch, N_FOX_HEADS, 1, seq)
    cos, sin = _rope_tables(seq)
    log_gamma = jnp.log1p(-(2.0 ** (-5.0 - jnp.arange(N_RET_HEADS, dtype=F32))))
    ret = retention(proj, cos, sin, log_gamma, batch)
    fox = fox_attention(proj, c_cols, c_rows, batch)
    m = gmm([ret, fox], [even_w_out], tm=512, tn=512, n_out=d, out_dtype=F32)
    x1, h1 = resid_norm(x0, m, even_norm_mix_post[0], even_norm_ffn_pre[0], "bf16")

    d_ff = even_w_gate.shape[-1]
    act = gmm(h1, [even_w_gate, even_w_up], tm=512, tn=256, n_out=d_ff, out_dtype=BF16)
    f = gmm(act, [even_w_down], tm=256, tn=256, n_out=d, out_dtype=F32)
    x2, h2 = resid_norm(x1, f, even_norm_ffn_post[0], odd_norm_mix_pre[0], "f32")

    pooled = pool(h2, batch)
    group = d // len(POOL_WINDOWS)
    tn_pool = 512
    per = group // tn_pool
    m2 = gmm(pooled, [odd_w_pool[0]], tm=512, tn=tn_pool, n_out=d, out_dtype=F32, k=group,
             scale=odd_pool_scale[0].reshape(1, d),
             a_col=lambda j: j // per, w_group=lambda j: j // per, w_col=lambda j: j % per)
    w_router = jnp.pad(odd_w_router[0], ((0, 0), (0, LANES - N_EXPERTS)))
    x3, h3, logits = resid_norm(x2, m2, odd_norm_mix_post[0], odd_norm_ffn_pre[0], "packed",
                                w_router=w_router)

    ids, gate, counts = route(logits)
    tm = MOE_TILE
    n_rows = 2 * n + N_EXPERTS * tm
    pos1, pos2, meta = _moe_layout(ids, counts, tm, n_rows // tm)
    xs = gather_rows(h3, pos1, pos2, n_rows)
    d_fe = odd_we_gate.shape[-1]
    act2 = gmm(xs, [odd_we_gate[0], odd_we_up[0]], tm=tm, tn=256, n_out=d_fe,
               out_dtype=BF16, meta=meta)
    o = gmm(act2, [odd_we_down[0]], tm=tm, tn=256, n_out=d, out_dtype=F32, meta=meta)
    out = combine(o, pos1, pos2, gate, x3, odd_norm_ffn_post[0])
    return out.reshape(batch, seq, d)
```

```python
import functools

import jax
import jax.numpy as jnp
from jax import lax
from jax.experimental import pallas as pl
from jax.experimental.pallas import tpu as pltpu

F32 = jnp.float32
BF16 = jnp.bfloat16
U32 = jnp.uint32
I32 = jnp.int32

HEAD_DIM = 128
N_RET_HEADS = 16
N_FOX_HEADS = 16
RET_CHUNK = 128
ROPE_BASE = 10000.0
N_EXPERTS = 8
POOL_WINDOWS = (2, 4, 8, 16)
EPS = 1e-6
LANES = 128
NEG_BIG = -1e30

VMEM_LIMIT = 56 * 1024 * 1024
MOE_TILE = 512
MOE_CHUNKS = 4


def _cparams(n_axes):
    return pltpu.CompilerParams(
        dimension_semantics=("arbitrary",) * n_axes, vmem_limit_bytes=VMEM_LIMIT)


def _rms(v):
    return lax.rsqrt(jnp.mean(v * v, axis=-1, keepdims=True) + EPS)


def _sigmoid(v):
    return 1.0 / (1.0 + jnp.exp(-v))


def _rms_cast_kernel(x_ref, g_ref, o_ref):
    x = x_ref[...]
    o_ref[...] = ((x * _rms(x)) * g_ref[...]).astype(o_ref.dtype)


def rms_cast(x, g, tm=256):
    n, d = x.shape
    return pl.pallas_call(
        _rms_cast_kernel,
        grid=(n // tm,),
        in_specs=[pl.BlockSpec((tm, d), lambda i: (i, 0)),
                  pl.BlockSpec((1, d), lambda i: (0, 0))],
        out_specs=pl.BlockSpec((tm, d), lambda i: (i, 0)),
        out_shape=jax.ShapeDtypeStruct((n, d), BF16),
        compiler_params=_cparams(1),
        name="rms_cast",
    )(x, g.reshape(1, d))


def _pack_bf16_pairs(h):
    half = h.shape[1] // 2
    bits = lax.bitcast_convert_type(h.astype(BF16).astype(F32), U32)
    return (bits[:, :half] >> 16) | (bits[:, half:] & jnp.uint32(0xFFFF0000))


def _unpack_bf16_pairs(p):
    lo = lax.bitcast_convert_type(p << 16, F32).astype(BF16)
    hi = lax.bitcast_convert_type(p & jnp.uint32(0xFFFF0000), F32).astype(BF16)
    return lo, hi


def _resid_norm_kernel(*refs, h_mode, router):
    x_ref, m_ref, gp_ref = refs[:3]
    pos = 3
    gn_ref = wr_ref = None
    if h_mode is not None:
        gn_ref = refs[pos]; pos += 1
    if router:
        wr_ref = refs[pos]; pos += 1
    xo_ref = refs[pos]; pos += 1
    m = m_ref[...]
    xn = x_ref[...] + (m * _rms(m)) * gp_ref[...]
    xo_ref[...] = xn
    if h_mode is None:
        return
    h = (xn * _rms(xn)) * gn_ref[...]
    h_ref = refs[pos]; pos += 1
    if h_mode == "packed":
        h_ref[...] = _pack_bf16_pairs(h)
    else:
        h_ref[...] = h.astype(h_ref.dtype)
    if router:
        refs[pos][...] = jnp.dot(h, wr_ref[...], preferred_element_type=F32,
                                 precision=lax.Precision.HIGHEST)


def resid_norm(x, m, g_post, g_next=None, h_mode=None, w_router=None, tm=256):
    n, d = x.shape
    row = pl.BlockSpec((tm, d), lambda i: (i, 0))
    vec = pl.BlockSpec((1, d), lambda i: (0, 0))
    args = [x, m, g_post.reshape(1, d)]
    in_specs = [row, row, vec]
    out_shape = [jax.ShapeDtypeStruct((n, d), F32)]
    out_specs = [row]
    if h_mode is not None:
        args.append(g_next.reshape(1, d)); in_specs.append(vec)
        if h_mode == "packed":
            out_shape.append(jax.ShapeDtypeStruct((n, d // 2), U32))
            out_specs.append(pl.BlockSpec((tm, d // 2), lambda i: (i, 0)))
        else:
            out_shape.append(jax.ShapeDtypeStruct((n, d), BF16 if h_mode == "bf16" else F32))
            out_specs.append(row)
    router = w_router is not None
    if router:
        args.append(w_router)
        in_specs.append(pl.BlockSpec((d, LANES), lambda i: (0, 0)))
        out_shape.append(jax.ShapeDtypeStruct((n, LANES), F32))
        out_specs.append(pl.BlockSpec((tm, LANES), lambda i: (i, 0)))
    return pl.pallas_call(
        functools.partial(_resid_norm_kernel, h_mode=h_mode, router=router),
        grid=(n // tm,),
        in_specs=in_specs, out_specs=out_specs, out_shape=out_shape,
        compiler_params=_cparams(1),
        name="resid_norm",
    )(*args)


N_SCHED = 9
KIND_NONE, KIND_COMPUTE, KIND_ZERO = 0, 1, 2


def _gmm_kernel(tile_ref, arow_ref, kind_ref, blk_ref, pfg_ref, pfw_ref, pfc_ref, pfd_ref,
                nblk_ref, *refs, n_a, n_w, n_col, has_scale, transposed, n_valid):
    del tile_ref, arow_ref, pfg_ref
    a_refs = refs[:n_a]
    c_refs = refs[n_a:n_a + n_w]
    pos = n_a + n_w
    s_ref = None
    if has_scale:
        s_ref = refs[pos]; pos += 1
    o_ref = refs[pos]; pos += 1
    wbf_refs = refs[pos:pos + n_w]
    jj = pl.program_id(0)
    s = pl.program_id(1)
    seq = jj * nblk_ref[0] + blk_ref[s] + nblk_ref[0]
    slot_cur = seq % 2
    slot_nxt = (seq + 1) % 2
    rows_per_chunk = c_refs[0].shape[1]
    tcol = jj - 1 + pfw_ref[s]

    @pl.when((pfd_ref[s] == 1) & (tcol >= 0) & (tcol < n_col))
    def _():
        r = pl.multiple_of(pfc_ref[s] * rows_per_chunk, rows_per_chunk)
        for c_ref, wbf in zip(c_refs, wbf_refs):
            wbf[slot_nxt, pl.ds(r, rows_per_chunk), :] = c_ref[0].astype(BF16)

    def matmul(wbf):
        if transposed:
            return lax.dot_general(a_refs[0][...], wbf[slot_cur], (((1,), (1,)), ((), ())),
                                   preferred_element_type=F32)
        acc = None
        off = 0
        for a_ref in a_refs:
            ka = a_ref.shape[1]
            part = jnp.dot(a_ref[...], wbf[slot_cur, off:off + ka, :],
                           preferred_element_type=F32)
            acc = part if acc is None else acc + part
            off += ka
        return acc

    @pl.when((kind_ref[s] == KIND_COMPUTE) & (jj >= 1))
    def _():
        acc = matmul(wbf_refs[0])
        if n_w == 2:
            acc = (acc * _sigmoid(acc)) * matmul(wbf_refs[1])
        if has_scale:
            acc = acc * s_ref[...]
        if n_valid is not None:
            col = lax.broadcasted_iota(I32, acc.shape, 1) + (jj - 1) * acc.shape[1]
            acc = jnp.where(col < n_valid, acc, 0.0)
        o_ref[...] = acc.astype(o_ref.dtype)

    @pl.when((kind_ref[s] == KIND_ZERO) & (jj >= 1))
    def _():
        o_ref[...] = jnp.zeros_like(o_ref)


def _dense_schedule(n_tiles, nc):
    t = jnp.arange(n_tiles, dtype=I32)
    zeros = jnp.zeros((n_tiles,), I32)
    ones = jnp.ones((n_tiles,), I32)
    return (t, t, ones, zeros, zeros, ones, jnp.minimum(t, nc - 1), (t < nc).astype(I32),
            jnp.ones((1,), I32))


def gmm(a, ws, *, tm, tn, n_out, out_dtype, nc, sched=None, k=None, scale=None,
        a_col=None, w_group=None, w_col=None, transposed=False, n_valid=None):
    a_parts = list(a) if isinstance(a, (list, tuple)) else [a]
    rows = a_parts[0].shape[0]
    if k is None:
        k = sum(p.shape[1] for p in a_parts)
    if sched is None:
        sched = _dense_schedule(rows // tm, nc)
    n_steps = sched[0].shape[0]
    n_col = pl.cdiv(n_out, tn)
    a_col = a_col or (lambda j: 0)
    w_col = w_col or (lambda j: j)

    def cur_col(jj):
        return jnp.maximum(jj - 1, 0)

    def a_map(jj, s, tile, arow, *_):
        return (jnp.where(jj == 0, arow[0], arow[s]), a_col(cur_col(jj)))

    def o_map(jj, s, tile, *_):
        return (jnp.where(jj == 0, tile[0], tile[s]), cur_col(jj))

    def c_map(jj, s, tile, arow, kind, blk, pfg, pfw, pfc, *_):
        col = jnp.clip(jj - 1 + pfw[s], 0, n_col - 1)
        group = pfg[s] if w_group is None else w_group(col)
        if transposed:
            return (group, w_col(col) * nc + pfc[s], 0)
        return (group, pfc[s], w_col(col))

    in_specs = [pl.BlockSpec((tm, k if len(a_parts) == 1 else p.shape[1]), a_map)
                for p in a_parts]
    if transposed:
        assert len(a_parts) == 1 and tn % nc == 0
        chunk, wbf_shape = (1, tn // nc, k), (2, tn, k)
    else:
        assert k % nc == 0
        chunk, wbf_shape = (1, k // nc, tn), (2, k, tn)
    in_specs += [pl.BlockSpec(chunk, c_map) for _ in ws]
    args = [*a_parts, *ws]
    if scale is not None:
        in_specs.append(pl.BlockSpec((1, tn), lambda jj, s, *_: (0, cur_col(jj))))
        args.append(scale)
    grid_spec = pltpu.PrefetchScalarGridSpec(
        num_scalar_prefetch=N_SCHED,
        grid=(n_col + 1, n_steps),
        in_specs=in_specs,
        out_specs=pl.BlockSpec((tm, tn), o_map),
        scratch_shapes=[pltpu.VMEM(wbf_shape, BF16) for _ in ws],
    )
    return pl.pallas_call(
        functools.partial(_gmm_kernel, n_a=len(a_parts), n_w=len(ws), n_col=n_col,
                          has_scale=scale is not None, transposed=transposed,
                          n_valid=n_valid),
        grid_spec=grid_spec,
        out_shape=jax.ShapeDtypeStruct((rows, n_out), out_dtype),
        compiler_params=_cparams(2),
        name="gmm",
    )(*sched, *args)


def _forget_cumsum_kernel(f_ref, b_ref, o_ref):
    x = f_ref[...] + b_ref[...]
    ls = jnp.minimum(x, 0.0) - jnp.log1p(jnp.exp(-jnp.abs(x)))
    s = ls.shape[0]
    row = lax.broadcasted_iota(I32, ls.shape, 0)
    sh = 1
    while sh < s:
        ls = ls + jnp.where(row >= sh, pltpu.roll(ls, sh, 0), 0.0)
        sh *= 2
    o_ref[...] = ls


def forget_cumsum(f, b, batch):
    n, w = f.shape
    s = n // batch
    return pl.pallas_call(
        _forget_cumsum_kernel,
        grid=(batch,),
        in_specs=[pl.BlockSpec((s, w), lambda i: (i, 0)),
                  pl.BlockSpec((1, w), lambda i: (0, 0))],
        out_specs=pl.BlockSpec((s, w), lambda i: (i, 0)),
        out_shape=jax.ShapeDtypeStruct((n, w), F32),
        compiler_params=_cparams(1),
        name="forget_cumsum",
    )(f, b)


def _retention_kernel(lg_ref, q_ref, k_ref, v_ref, g_ref, cos_ref, sin_ref, o_ref):
    c = RET_CHUNK
    d = HEAD_DIM
    s = q_ref.shape[0]
    lg = lg_ref[pl.program_id(1)]
    ii = lax.broadcasted_iota(I32, (c, c), 0)
    jj = lax.broadcasted_iota(I32, (c, c), 1)
    diff = (ii - jj).astype(F32)
    decay_in = jnp.where(diff >= 0, jnp.exp(lg * jnp.maximum(diff, 0.0)), 0.0)
    jc = lax.broadcasted_iota(I32, (c, 1), 0).astype(F32)
    zeta = jnp.exp(lg * (c - 1 - jc))
    q_decay = jnp.exp(lg * (jc + 1.0))
    gamma_c = jnp.exp(jnp.full((1, 1), lg, F32) * c)
    scale = d ** -0.5
    nt = (((1,), (1,)), ((), ()))
    tn = (((0,), (0,)), ((), ()))
    state = jnp.zeros((d, d), F32)
    for n in range(s // c):
        sl = pl.ds(n * c, c)
        cos = cos_ref[sl, :]
        sin = sin_ref[sl, :]
        q = q_ref[sl, :].astype(F32)
        k = k_ref[sl, :].astype(F32)
        v = v_ref[sl, :]
        qr = q * cos + pltpu.roll(q, d // 2, 1) * sin
        kr = (k * cos + pltpu.roll(k, d // 2, 1) * sin) * scale
        scores = lax.dot_general(qr.astype(BF16), kr.astype(BF16), nt,
                                 preferred_element_type=F32) * decay_in
        inner = jnp.dot(scores.astype(BF16), v, preferred_element_type=F32)
        cross = jnp.dot((qr * q_decay).astype(BF16), state.astype(BF16),
                        preferred_element_type=F32)
        kv = lax.dot_general((kr * zeta).astype(BF16), v, tn, preferred_element_type=F32)
        state = gamma_c * state + kv
        o = inner + cross
        xc = o - jnp.mean(o, axis=-1, keepdims=True)
        y = xc * lax.rsqrt(jnp.mean(xc * xc, axis=-1, keepdims=True) + EPS)
        g = g_ref[sl, :].astype(F32)
        o_ref[sl, :] = ((g * _sigmoid(g)) * y).astype(o_ref.dtype)


def retention(proj, cos, sin, log_gamma, batch):
    n = proj.shape[0]
    s = n // batch
    d = HEAD_DIM
    hh = N_RET_HEADS

    def col(off):
        return pl.BlockSpec((s, d), lambda b, h, lg: (b, off + h))

    tab = pl.BlockSpec((s, d), lambda b, h, lg: (0, 0))
    grid_spec = pltpu.PrefetchScalarGridSpec(
        num_scalar_prefetch=1,
        grid=(batch, hh),
        in_specs=[col(0), col(hh), col(2 * hh), col(3 * hh), tab, tab],
        out_specs=pl.BlockSpec((s, d), lambda b, h, lg: (b, h)),
    )
    return pl.pallas_call(
        _retention_kernel,
        grid_spec=grid_spec,
        out_shape=jax.ShapeDtypeStruct((n, hh * d), BF16),
        compiler_params=_cparams(2),
        name="retention",
    )(log_gamma, proj, proj, proj, proj, cos, sin)


def _fox_kernel(q_ref, k_ref, v_ref, ccol_ref, crow_ref, o_ref, *, tq):
    s, d = q_ref.shape
    h = pl.program_id(1)
    scale = d ** -0.5
    nt = (((1,), (1,)), ((), ()))
    lane = lax.broadcasted_iota(I32, (tq, LANES), 1)
    rr = lax.broadcasted_iota(I32, (tq, tq), 0)
    cc = lax.broadcasted_iota(I32, (tq, tq), 1)

    for qi in range(s // tq):
        qs = pl.ds(qi * tq, tq)
        q = q_ref[qs, :]
        cq = jnp.sum(jnp.where(lane == h, ccol_ref[qs, :], 0.0), axis=-1, keepdims=True)

        def scores(kb):
            ks = pl.ds(pl.multiple_of(kb * tq, tq), tq)
            sc = lax.dot_general(q, k_ref[ks, :], nt, preferred_element_type=F32) * scale
            return sc + cq - crow_ref[:, ks], v_ref[ks, :]

        def update(carry, sc, v):
            m, l, acc = carry
            m_new = jnp.maximum(m, jnp.max(sc, axis=-1, keepdims=True))
            alpha = jnp.exp(m - m_new)
            p = jnp.exp(sc - m_new)
            l = alpha * l + jnp.sum(p, axis=-1, keepdims=True)
            acc = alpha * acc + jnp.dot(p.astype(BF16), v, preferred_element_type=F32)
            return m_new, l, acc

        def body(kb, carry):
            sc, v = scores(kb)
            return update(carry, sc, v)

        carry = (jnp.full((tq, 1), NEG_BIG, F32), jnp.zeros((tq, 1), F32),
                 jnp.zeros((tq, d), F32))
        carry = lax.fori_loop(0, qi, body, carry, unroll=True)
        sc, v = scores(qi)
        _, l, acc = update(carry, jnp.where(rr >= cc, sc, NEG_BIG), v)
        o_ref[qs, :] = (acc / l).astype(o_ref.dtype)


def fox_attention(proj, c_cols, c_rows, batch, tq=256):
    n = proj.shape[0]
    s = n // batch
    d = HEAD_DIM
    base = 4 * N_RET_HEADS
    hh = N_FOX_HEADS

    def col(off):
        return pl.BlockSpec((s, d), lambda b, h: (b, off + h))

    return pl.pallas_call(
        functools.partial(_fox_kernel, tq=tq),
        grid=(batch, hh),
        in_specs=[col(base), col(base + hh), col(base + 2 * hh),
                  pl.BlockSpec((s, LANES), lambda b, h: (b, 0)),
                  pl.BlockSpec((None, None, 1, s), lambda b, h: (b, h, 0, 0))],
        out_specs=pl.BlockSpec((s, d), lambda b, h: (b, h)),
        out_shape=jax.ShapeDtypeStruct((n, hh * d), BF16),
        compiler_params=_cparams(2),
        name="fox_attention",
    )(proj, proj, proj, c_cols, c_rows)


def _pool_kernel(h_ref, o_ref, *, blocks_per_group):
    x = h_ref[...]
    row = lax.broadcasted_iota(I32, x.shape, 0)
    group = pl.program_id(1) // blocks_per_group

    def shifted(a, by):
        return jnp.where(row >= by, pltpu.roll(a, by, 0), 0.0)

    for gi, w in enumerate(POOL_WINDOWS):
        @pl.when(group == gi)
        def _():
            acc = x
            by = 1
            while by < w:
                acc = acc + shifted(acc, by)
                by *= 2
            count = jnp.minimum(row + 1, w).astype(F32)
            o_ref[...] = (acc / count - x).astype(o_ref.dtype)


def pool(h, batch, tc=256):
    n, d = h.shape
    s = n // batch
    group = d // len(POOL_WINDOWS)
    return pl.pallas_call(
        functools.partial(_pool_kernel, blocks_per_group=group // tc),
        grid=(batch, d // tc),
        in_specs=[pl.BlockSpec((s, tc), lambda b, j: (b, j))],
        out_specs=pl.BlockSpec((s, tc), lambda b, j: (b, j)),
        out_shape=jax.ShapeDtypeStruct((n, d), BF16),
        compiler_params=_cparams(2),
        name="pool",
    )(h)


def _route_kernel(lg_ref, ids_ref, gate_ref, cnt_ref, carry_ref):
    tt = lg_ref.shape[0]

    @pl.when(pl.program_id(0) == 0)
    def _():
        carry_ref[...] = jnp.zeros_like(carry_ref)

    lane = lax.broadcasted_iota(I32, (tt, LANES), 1)
    lane_f = lane.astype(F32)
    x = jnp.where(lane < N_EXPERTS, lg_ref[...], -jnp.inf)
    v1 = jnp.max(x, axis=-1, keepdims=True)
    i1 = jnp.min(jnp.where(x == v1, lane_f, float(LANES)), axis=-1, keepdims=True)
    x2 = jnp.where(lane_f == i1, -jnp.inf, x)
    v2 = jnp.max(x2, axis=-1, keepdims=True)
    i2 = jnp.min(jnp.where(x2 == v2, lane_f, float(LANES)), axis=-1, keepdims=True)
    e = jnp.exp(v2 - v1)
    w1 = 1.0 / (1.0 + e)
    w2 = e / (1.0 + e)
    sel1 = lane_f == i1
    sel2 = lane_f == i2
    onehot = jnp.where(sel1, 1.0, jnp.where(sel2, 1.0, 0.0))
    rr = lax.broadcasted_iota(I32, (tt, tt), 0)
    cc = lax.broadcasted_iota(I32, (tt, tt), 1)
    tri = jnp.where(rr > cc, 1.0, 0.0).astype(BF16)
    before = jnp.dot(tri, onehot.astype(BF16), preferred_element_type=F32) + carry_ref[0:1, :]
    r1 = jnp.sum(jnp.where(sel1, before, 0.0), axis=-1, keepdims=True)
    r2 = jnp.sum(jnp.where(sel2, before, 0.0), axis=-1, keepdims=True)
    ids = jnp.where(lane == 0, i1, jnp.where(lane == 1, i2,
                    jnp.where(lane == 2, r1, jnp.where(lane == 3, r2, 0.0))))
    ids_ref[...] = ids.astype(I32)
    gate_ref[...] = jnp.where(lane == 0, w1, jnp.where(lane == 1, w2, 0.0))
    total = carry_ref[0:1, :] + jnp.sum(onehot, axis=0, keepdims=True)
    carry_ref[...] = jnp.broadcast_to(total, carry_ref.shape)
    cnt_ref[...] = jnp.broadcast_to(total, cnt_ref.shape).astype(I32)


def route(logits, tt=256):
    n = logits.shape[0]
    row = pl.BlockSpec((tt, LANES), lambda i: (i, 0))
    return pl.pallas_call(
        _route_kernel,
        grid=(n // tt,),
        in_specs=[row],
        out_specs=[row, row, pl.BlockSpec((8, LANES), lambda i: (0, 0))],
        out_shape=[jax.ShapeDtypeStruct((n, LANES), I32),
                   jax.ShapeDtypeStruct((n, LANES), F32),
                   jax.ShapeDtypeStruct((8, LANES), I32)],
        scratch_shapes=[pltpu.VMEM((8, LANES), F32)],
        compiler_params=_cparams(1),
        name="route",
    )(logits)


def _gather_kernel(pos1_ref, pos2_ref, h_hbm, o_ref, src_ref, buf_ref, sem, *, n_tokens):
    tg, half = buf_ref.shape
    i = pl.program_id(0)

    @pl.when(i == 0)
    def _():
        def clear(p, carry):
            src_ref[p] = 0
            return carry
        lax.fori_loop(0, src_ref.shape[0], clear, 0)

        def place(t, carry):
            src_ref[pos1_ref[t]] = t
            src_ref[pos2_ref[t]] = t
            return carry
        lax.fori_loop(0, n_tokens, place, 0)

    def row_copy(r):
        return pltpu.make_async_copy(
            h_hbm.at[pl.ds(src_ref[i * tg + r], 1), :], buf_ref.at[pl.ds(r, 1), :], sem)

    def issue(r, carry):
        row_copy(r).start()
        return carry
    lax.fori_loop(0, tg, issue, 0)

    def drain(r, carry):
        row_copy(r).wait()
        return carry
    lax.fori_loop(0, tg, drain, 0)

    lo, hi = _unpack_bf16_pairs(buf_ref[...])
    o_ref[:, :half] = lo
    o_ref[:, half:] = hi


def gather_rows(h_packed, pos1, pos2, n_rows, tg=256):
    n, half = h_packed.shape
    grid_spec = pltpu.PrefetchScalarGridSpec(
        num_scalar_prefetch=2,
        grid=(n_rows // tg,),
        in_specs=[pl.BlockSpec(memory_space=pl.ANY)],
        out_specs=pl.BlockSpec((tg, 2 * half), lambda i, p1, p2: (i, 0)),
        scratch_shapes=[pltpu.SMEM((n_rows,), I32), pltpu.VMEM((tg, half), U32),
                        pltpu.SemaphoreType.DMA(())],
    )
    return pl.pallas_call(
        functools.partial(_gather_kernel, n_tokens=n),
        grid_spec=grid_spec,
        out_shape=jax.ShapeDtypeStruct((n_rows, 2 * half), BF16),
        compiler_params=_cparams(1),
        name="gather_rows",
    )(pos1, pos2, h_packed)


def _combine_kernel(pos1_ref, pos2_ref, o_hbm, gate_ref, x_ref, g_ref, out_ref, buf_ref, sem):
    tt = x_ref.shape[0]
    i = pl.program_id(0)

    def row_copy(r, slot, pos_ref):
        return pltpu.make_async_copy(
            o_hbm.at[pl.ds(pos_ref[i * tt + r], 1), :], buf_ref.at[slot, pl.ds(r, 1), :], sem)

    def issue(r, carry):
        row_copy(r, 0, pos1_ref).start()
        row_copy(r, 1, pos2_ref).start()
        return carry
    lax.fori_loop(0, tt, issue, 0)

    def drain(r, carry):
        row_copy(r, 0, pos1_ref).wait()
        row_copy(r, 1, pos2_ref).wait()
        return carry
    lax.fori_loop(0, tt, drain, 0)

    gate = gate_ref[...]
    y = gate[:, 0:1] * buf_ref[0] + gate[:, 1:2] * buf_ref[1]
    out_ref[...] = x_ref[...] + (y * _rms(y)) * g_ref[...]


def combine(o, pos1, pos2, gate, x, g, tt=128):
    n, d = x.shape
    row = lambda i, p1, p2: (i, 0)
    grid_spec = pltpu.PrefetchScalarGridSpec(
        num_scalar_prefetch=2,
        grid=(n // tt,),
        in_specs=[pl.BlockSpec(memory_space=pl.ANY),
                  pl.BlockSpec((tt, LANES), row),
                  pl.BlockSpec((tt, d), row),
                  pl.BlockSpec((1, d), lambda i, p1, p2: (0, 0))],
        out_specs=pl.BlockSpec((tt, d), row),
        scratch_shapes=[pltpu.VMEM((2, tt, d), F32), pltpu.SemaphoreType.DMA(())],
    )
    return pl.pallas_call(
        _combine_kernel,
        grid_spec=grid_spec,
        out_shape=jax.ShapeDtypeStruct((n, d), F32),
        compiler_params=_cparams(1),
        name="combine",
    )(pos1, pos2, o, gate, x, g.reshape(1, d))


def _rope_tables(s):
    half = HEAD_DIM // 2
    inv = ROPE_BASE ** (-jnp.arange(half, dtype=F32) / half)
    ang = jnp.arange(s, dtype=F32)[:, None] * inv[None, :]
    cos, sin = jnp.cos(ang), jnp.sin(ang)
    return jnp.concatenate([cos, cos], axis=-1), jnp.concatenate([-sin, sin], axis=-1)


def _moe_schedule(ids, counts, tm, n_tiles, nc):
    ne = N_EXPERTS
    counts = counts[0, :ne]
    tiles = (counts + tm - 1) // tm
    nonempty = tiles > 0
    tile_end = jnp.cumsum(tiles)
    tile_start = tile_end - tiles
    pos1 = tile_start[ids[:, 0]] * tm + ids[:, 2]
    pos2 = tile_start[ids[:, 1]] * tm + ids[:, 3]

    steps = jnp.where(nonempty, jnp.maximum(tiles, nc), 0)
    step_end = jnp.cumsum(steps)
    step_start = step_end - steps
    total, n_used = step_end[-1], tile_end[-1]
    ar = jnp.arange(ne, dtype=I32)
    first_e = jnp.min(jnp.where(nonempty, ar, ne))
    last_e = jnp.max(jnp.where(nonempty, ar, -1))
    later = (ar[None, :] > ar[:, None]) & nonempty[None, :]
    nxt = jnp.min(jnp.where(later, ar[None, :], ne), axis=1)
    wrap = nxt == ne
    nxt = jnp.where(wrap, first_e, nxt)
    rank = jnp.cumsum(nonempty.astype(I32)) - 1

    s = jnp.arange(n_tiles + ne * (nc - 1), dtype=I32)
    in_blocks = s < total
    e = jnp.minimum(jnp.sum((s[:, None] >= step_end[None, :]).astype(I32), axis=1), ne - 1)
    e = jnp.where(in_blocks, e, last_e)
    p = s - step_start[e]
    is_comp = in_blocks & (p < tiles[e])
    z = s - total
    is_zero = (~in_blocks) & (z < n_tiles - n_used)
    group_last = tile_start[e] + tiles[e] - 1
    tile = jnp.where(is_comp, tile_start[e] + p,
                     jnp.where(is_zero, n_used + z, jnp.where(in_blocks, group_last, n_tiles - 1)))
    arow = jnp.where(is_comp, tile, jnp.where(in_blocks, group_last, n_used - 1))
    kind = jnp.where(is_comp, KIND_COMPUTE, jnp.where(is_zero, KIND_ZERO, KIND_NONE))
    pf_chunk = jnp.where(in_blocks, jnp.minimum(p, nc - 1), nc - 1)
    pf_do = in_blocks & (p < nc)
    sched = (tile, arow, kind, rank[e], nxt[e], wrap[e], pf_chunk, pf_do,
             jnp.sum(nonempty.astype(I32)).reshape(1))
    return pos1.astype(I32), pos2.astype(I32), tuple(v.astype(I32) for v in sched)


def kernel(x, even_norm_mix_pre, even_w_in, even_b_forget, even_w_out, even_norm_mix_post,
           even_norm_ffn_pre, even_w_gate, even_w_up, even_w_down, even_norm_ffn_post,
           odd_norm_mix_pre, odd_w_pool, odd_pool_scale, odd_norm_mix_post, odd_norm_ffn_pre,
           odd_w_router, odd_we_gate, odd_we_up, odd_we_down, odd_norm_ffn_post):
    batch, seq, d = x.shape
    n = batch * seq
    x0 = x.reshape(n, d)
    ret_w = N_RET_HEADS * HEAD_DIM
    fox_w = N_FOX_HEADS * HEAD_DIM
    main_cols = 4 * ret_w + 3 * fox_w

    h0 = rms_cast(x0, even_norm_mix_pre[0])
    w_in_t = jnp.swapaxes(even_w_in, 1, 2)
    proj = gmm(h0, [w_in_t], tm=512, tn=1024, n_out=main_cols, out_dtype=BF16, nc=4,
               transposed=True)
    b_forget = jnp.pad(even_b_forget[0], (0, LANES - N_FOX_HEADS)).reshape(1, LANES)
    f_logit = gmm(h0, [w_in_t], tm=512, tn=LANES, n_out=LANES, out_dtype=F32, nc=1,
                  transposed=True, w_col=lambda j: main_cols // LANES + j,
                  n_valid=N_FOX_HEADS)
    c_cols = forget_cumsum(f_logit, b_forget, batch)
    c_rows = c_cols[:, :N_FOX_HEADS].reshape(batch, seq, N_FOX_HEADS)
    c_rows = c_rows.transpose(0, 2, 1).reshape(batch, N_FOX_HEADS, 1, seq)
    cos, sin = _rope_tables(seq)
    log_gamma = jnp.log1p(-(2.0 ** (-5.0 - jnp.arange(N_RET_HEADS, dtype=F32))))
    ret = retention(proj, cos, sin, log_gamma, batch)
    fox = fox_attention(proj, c_cols, c_rows, batch)
    m = gmm([ret, fox], [even_w_out], tm=512, tn=1024, n_out=d, out_dtype=F32, nc=8)
    x1, h1 = resid_norm(x0, m, even_norm_mix_post[0], even_norm_ffn_pre[0], "bf16")

    d_ff = even_w_gate.shape[-1]
    act = gmm(h1, [even_w_gate, even_w_up], tm=512, tn=512, n_out=d_ff, out_dtype=BF16, nc=4)
    f = gmm(act, [even_w_down], tm=256, tn=512, n_out=d, out_dtype=F32, nc=8)
    x2, h2 = resid_norm(x1, f, even_norm_ffn_post[0], odd_norm_mix_pre[0], "f32")

    pooled = pool(h2, batch)
    group = d // len(POOL_WINDOWS)
    tn_pool = 512
    per = group // tn_pool
    m2 = gmm(pooled, [odd_w_pool[0]], tm=512, tn=tn_pool, n_out=d, out_dtype=F32, nc=2,
             k=group, scale=odd_pool_scale[0].reshape(1, d),
             a_col=lambda j: j // per, w_group=lambda j: j // per, w_col=lambda j: j % per)
    w_router = jnp.pad(odd_w_router[0], ((0, 0), (0, LANES - N_EXPERTS)))
    x3, h3, logits = resid_norm(x2, m2, odd_norm_mix_post[0], odd_norm_ffn_pre[0], "packed",
                                w_router=w_router)

    ids, gate, counts = route(logits)
    tm = MOE_TILE
    n_rows = 2 * n + N_EXPERTS * tm
    pos1, pos2, sched = _moe_schedule(ids, counts, tm, n_rows // tm, MOE_CHUNKS)
    xs = gather_rows(h3, pos1, pos2, n_rows)
    d_fe = odd_we_gate.shape[-1]
    act2 = gmm(xs, [odd_we_gate[0], odd_we_up[0]], tm=tm, tn=512, n_out=d_fe,
               out_dtype=BF16, nc=MOE_CHUNKS, sched=sched)
    o = gmm(act2, [odd_we_down[0]], tm=tm, tn=512, n_out=d, out_dtype=F32,
            nc=MOE_CHUNKS, sched=sched)
    out = combine(o, pos1, pos2, gate, x3, odd_norm_ffn_post[0])
    return out.reshape(batch, seq, d)
```

```python
import functools

import jax
import jax.numpy as jnp
from jax import lax
from jax.experimental import pallas as pl
from jax.experimental.pallas import tpu as pltpu

F32 = jnp.float32
BF16 = jnp.bfloat16
U32 = jnp.uint32
I32 = jnp.int32

HEAD_DIM = 128
N_RET_HEADS = 16
N_FOX_HEADS = 16
RET_CHUNK = 128
ROPE_BASE = 10000.0
N_EXPERTS = 8
POOL_WINDOWS = (2, 4, 8, 16)
EPS = 1e-6
LANES = 128
NEG_BIG = -1e30

VMEM_LIMIT = 56 * 1024 * 1024
MOE_TILE = 512
MOE_CHUNKS = 4


def _cparams(n_axes):
    return pltpu.CompilerParams(
        dimension_semantics=("arbitrary",) * n_axes, vmem_limit_bytes=VMEM_LIMIT)


def _rms(v):
    return lax.rsqrt(jnp.mean(v * v, axis=-1, keepdims=True) + EPS)


def _sigmoid(v):
    return 1.0 / (1.0 + jnp.exp(-v))


def _rms_cast_kernel(x_ref, g_ref, o_ref):
    x = x_ref[...]
    o_ref[...] = ((x * _rms(x)) * g_ref[...]).astype(o_ref.dtype)


def rms_cast(x, g, tm=256):
    n, d = x.shape
    return pl.pallas_call(
        _rms_cast_kernel,
        grid=(n // tm,),
        in_specs=[pl.BlockSpec((tm, d), lambda i: (i, 0)),
                  pl.BlockSpec((1, d), lambda i: (0, 0))],
        out_specs=pl.BlockSpec((tm, d), lambda i: (i, 0)),
        out_shape=jax.ShapeDtypeStruct((n, d), BF16),
        compiler_params=_cparams(1),
        name="rms_cast",
    )(x, g.reshape(1, d))


def _pack_bf16_pairs(h):
    half = h.shape[1] // 2
    bits = lax.bitcast_convert_type(h.astype(BF16).astype(F32), U32)
    return (bits[:, :half] >> 16) | (bits[:, half:] & jnp.uint32(0xFFFF0000))


def _unpack_bf16_pairs(p):
    lo = lax.bitcast_convert_type(p << 16, F32).astype(BF16)
    hi = lax.bitcast_convert_type(p & jnp.uint32(0xFFFF0000), F32).astype(BF16)
    return lo, hi


def _resid_norm_kernel(*refs, h_mode, router):
    x_ref, m_ref, gp_ref = refs[:3]
    pos = 3
    gn_ref = wr_ref = None
    if h_mode is not None:
        gn_ref = refs[pos]; pos += 1
    if router:
        wr_ref = refs[pos]; pos += 1
    xo_ref = refs[pos]; pos += 1
    m = m_ref[...]
    xn = x_ref[...] + (m * _rms(m)) * gp_ref[...]
    xo_ref[...] = xn
    if h_mode is None:
        return
    h = (xn * _rms(xn)) * gn_ref[...]
    h_ref = refs[pos]; pos += 1
    if h_mode == "packed":
        packed = _pack_bf16_pairs(h)
        nb = packed.shape[1] // LANES
        for c in range(nb):
            h_ref[pl.ds(c, packed.shape[0], stride=nb), :] = packed[:, c * LANES:(c + 1) * LANES]
    else:
        h_ref[...] = h.astype(h_ref.dtype)
    if router:
        refs[pos][...] = jnp.dot(h, wr_ref[...], preferred_element_type=F32,
                                 precision=lax.Precision.HIGHEST)


def resid_norm(x, m, g_post, g_next=None, h_mode=None, w_router=None, tm=256):
    n, d = x.shape
    row = pl.BlockSpec((tm, d), lambda i: (i, 0))
    vec = pl.BlockSpec((1, d), lambda i: (0, 0))
    args = [x, m, g_post.reshape(1, d)]
    in_specs = [row, row, vec]
    out_shape = [jax.ShapeDtypeStruct((n, d), F32)]
    out_specs = [row]
    if h_mode is not None:
        args.append(g_next.reshape(1, d)); in_specs.append(vec)
        if h_mode == "packed":
            nb = d // 2 // LANES
            out_shape.append(jax.ShapeDtypeStruct((n * nb, LANES), U32))
            out_specs.append(pl.BlockSpec((tm * nb, LANES), lambda i: (i, 0)))
        else:
            out_shape.append(jax.ShapeDtypeStruct((n, d), BF16 if h_mode == "bf16" else F32))
            out_specs.append(row)
    router = w_router is not None
    if router:
        args.append(w_router)
        in_specs.append(pl.BlockSpec((d, LANES), lambda i: (0, 0)))
        out_shape.append(jax.ShapeDtypeStruct((n, LANES), F32))
        out_specs.append(pl.BlockSpec((tm, LANES), lambda i: (i, 0)))
    return pl.pallas_call(
        functools.partial(_resid_norm_kernel, h_mode=h_mode, router=router),
        grid=(n // tm,),
        in_specs=in_specs, out_specs=out_specs, out_shape=out_shape,
        compiler_params=_cparams(1),
        name="resid_norm",
    )(*args)


N_SCHED = 9
KIND_NONE, KIND_COMPUTE, KIND_ZERO = 0, 1, 2


def _gmm_kernel(tile_ref, arow_ref, kind_ref, blk_ref, pfg_ref, pfw_ref, pfc_ref, pfd_ref,
                nblk_ref, *refs, n_a, n_w, n_col, has_scale, transposed, n_valid):
    del tile_ref, arow_ref, pfg_ref
    a_refs = refs[:n_a]
    c_refs = refs[n_a:n_a + n_w]
    pos = n_a + n_w
    s_ref = None
    if has_scale:
        s_ref = refs[pos]; pos += 1
    o_ref = refs[pos]; pos += 1
    slots = (refs[pos:pos + n_w], refs[pos + n_w:pos + 2 * n_w])
    jj = pl.program_id(0)
    s = pl.program_id(1)
    parity = (jj * nblk_ref[0] + blk_ref[s] + nblk_ref[0]) % 2
    rows_per_chunk = c_refs[0].shape[1]
    tcol = jj - 1 + pfw_ref[s]
    chunk_due = (pfd_ref[s] == 1) & (tcol >= 0) & (tcol < n_col)
    computing = (kind_ref[s] == KIND_COMPUTE) & (jj >= 1)

    def cast_chunk(nxt):
        r = pl.multiple_of(pfc_ref[s] * rows_per_chunk, rows_per_chunk)
        for c_ref, wbf in zip(c_refs, nxt):
            wbf[pl.ds(r, rows_per_chunk), :] = c_ref[0].astype(BF16)

    def matmul(wbf):
        if transposed:
            return lax.dot_general(a_refs[0][...], wbf[...], (((1,), (1,)), ((), ())),
                                   preferred_element_type=F32)
        acc = None
        off = 0
        for a_ref in a_refs:
            ka = a_ref.shape[1]
            part = jnp.dot(a_ref[...], wbf[off:off + ka, :], preferred_element_type=F32)
            acc = part if acc is None else acc + part
            off += ka
        return acc

    def compute(cur, nxt):
        cast_chunk(nxt)
        acc = matmul(cur[0])
        if n_w == 2:
            acc = (acc * _sigmoid(acc)) * matmul(cur[1])
        if has_scale:
            acc = acc * s_ref[...]
        if n_valid is not None:
            col = lax.broadcasted_iota(I32, acc.shape, 1) + (jj - 1) * acc.shape[1]
            acc = jnp.where(col < n_valid, acc, 0.0)
        o_ref[...] = acc.astype(o_ref.dtype)

    for p in (0, 1):
        cur, nxt = slots[p], slots[1 - p]
        pl.when(computing & (parity == p))(functools.partial(compute, cur, nxt))
        pl.when(chunk_due & jnp.logical_not(computing) & (parity == p))(
            functools.partial(cast_chunk, nxt))

    @pl.when((kind_ref[s] == KIND_ZERO) & (jj >= 1))
    def _():
        o_ref[...] = jnp.zeros_like(o_ref)


def _dense_schedule(n_tiles, nc):
    t = jnp.arange(n_tiles, dtype=I32)
    zeros = jnp.zeros((n_tiles,), I32)
    ones = jnp.ones((n_tiles,), I32)
    return (t, t, ones, zeros, zeros, ones, jnp.minimum(t, nc - 1), (t < nc).astype(I32),
            jnp.ones((1,), I32))


def gmm(a, ws, *, tm, tn, n_out, out_dtype, nc, sched=None, k=None, scale=None,
        a_col=None, w_group=None, w_col=None, transposed=False, n_valid=None):
    a_parts = list(a) if isinstance(a, (list, tuple)) else [a]
    rows = a_parts[0].shape[0]
    if k is None:
        k = sum(p.shape[1] for p in a_parts)
    if sched is None:
        sched = _dense_schedule(rows // tm, nc)
    n_steps = sched[0].shape[0]
    n_col = pl.cdiv(n_out, tn)
    a_col = a_col or (lambda j: 0)
    w_col = w_col or (lambda j: j)

    def cur_col(jj):
        return jnp.maximum(jj - 1, 0)

    def a_map(jj, s, tile, arow, *_):
        return (jnp.where(jj == 0, arow[0], arow[s]), a_col(cur_col(jj)))

    def o_map(jj, s, tile, *_):
        return (jnp.where(jj == 0, tile[0], tile[s]), cur_col(jj))

    def c_map(jj, s, tile, arow, kind, blk, pfg, pfw, pfc, *_):
        col = jnp.clip(jj - 1 + pfw[s], 0, n_col - 1)
        group = pfg[s] if w_group is None else w_group(col)
        if transposed:
            return (group, w_col(col) * nc + pfc[s], 0)
        return (group, pfc[s], w_col(col))

    in_specs = [pl.BlockSpec((tm, k if len(a_parts) == 1 else p.shape[1]), a_map)
                for p in a_parts]
    if transposed:
        assert len(a_parts) == 1 and tn % nc == 0
        chunk, wbf_shape = (1, tn // nc, k), (tn, k)
    else:
        assert k % nc == 0
        chunk, wbf_shape = (1, k // nc, tn), (k, tn)
    in_specs += [pl.BlockSpec(chunk, c_map) for _ in ws]
    args = [*a_parts, *ws]
    if scale is not None:
        in_specs.append(pl.BlockSpec((1, tn), lambda jj, s, *_: (0, cur_col(jj))))
        args.append(scale)
    grid_spec = pltpu.PrefetchScalarGridSpec(
        num_scalar_prefetch=N_SCHED,
        grid=(n_col + 1, n_steps),
        in_specs=in_specs,
        out_specs=pl.BlockSpec((tm, tn), o_map),
        scratch_shapes=[pltpu.VMEM(wbf_shape, BF16) for _ in range(2 * len(ws))],
    )
    return pl.pallas_call(
        functools.partial(_gmm_kernel, n_a=len(a_parts), n_w=len(ws), n_col=n_col,
                          has_scale=scale is not None, transposed=transposed,
                          n_valid=n_valid),
        grid_spec=grid_spec,
        out_shape=jax.ShapeDtypeStruct((rows, n_out), out_dtype),
        compiler_params=_cparams(2),
        name="gmm",
    )(*sched, *args)


def _forget_cumsum_kernel(f_ref, b_ref, o_ref):
    x = f_ref[...] + b_ref[...]
    ls = jnp.minimum(x, 0.0) - jnp.log1p(jnp.exp(-jnp.abs(x)))
    s = ls.shape[0]
    row = lax.broadcasted_iota(I32, ls.shape, 0)
    sh = 1
    while sh < s:
        ls = ls + jnp.where(row >= sh, pltpu.roll(ls, sh, 0), 0.0)
        sh *= 2
    o_ref[...] = ls


def forget_cumsum(f, b, batch):
    n, w = f.shape
    s = n // batch
    return pl.pallas_call(
        _forget_cumsum_kernel,
        grid=(batch,),
        in_specs=[pl.BlockSpec((s, w), lambda i: (i, 0)),
                  pl.BlockSpec((1, w), lambda i: (0, 0))],
        out_specs=pl.BlockSpec((s, w), lambda i: (i, 0)),
        out_shape=jax.ShapeDtypeStruct((n, w), F32),
        compiler_params=_cparams(1),
        name="forget_cumsum",
    )(f, b)


def _retention_kernel(lg_ref, q_ref, k_ref, v_ref, g_ref, cos_ref, sin_ref, o_ref):
    c = RET_CHUNK
    d = HEAD_DIM
    s = q_ref.shape[0]
    lg = lg_ref[pl.program_id(1)]
    ii = lax.broadcasted_iota(I32, (c, c), 0)
    jj = lax.broadcasted_iota(I32, (c, c), 1)
    diff = (ii - jj).astype(F32)
    decay_in = jnp.where(diff >= 0, jnp.exp(lg * jnp.maximum(diff, 0.0)), 0.0)
    jc = lax.broadcasted_iota(I32, (c, 1), 0).astype(F32)
    zeta = jnp.exp(lg * (c - 1 - jc))
    q_decay = jnp.exp(lg * (jc + 1.0))
    gamma_c = jnp.exp(jnp.full((1, 1), lg, F32) * c)
    scale = d ** -0.5
    nt = (((1,), (1,)), ((), ()))
    tn = (((0,), (0,)), ((), ()))
    state = jnp.zeros((d, d), F32)
    for n in range(s // c):
        sl = pl.ds(n * c, c)
        cos = cos_ref[sl, :]
        sin = sin_ref[sl, :]
        q = q_ref[sl, :].astype(F32)
        k = k_ref[sl, :].astype(F32)
        v = v_ref[sl, :]
        qr = q * cos + pltpu.roll(q, d // 2, 1) * sin
        kr = (k * cos + pltpu.roll(k, d // 2, 1) * sin) * scale
        scores = lax.dot_general(qr.astype(BF16), kr.astype(BF16), nt,
                                 preferred_element_type=F32) * decay_in
        inner = jnp.dot(scores.astype(BF16), v, preferred_element_type=F32)
        cross = jnp.dot((qr * q_decay).astype(BF16), state.astype(BF16),
                        preferred_element_type=F32)
        kv = lax.dot_general((kr * zeta).astype(BF16), v, tn, preferred_element_type=F32)
        state = gamma_c * state + kv
        o = inner + cross
        xc = o - jnp.mean(o, axis=-1, keepdims=True)
        y = xc * lax.rsqrt(jnp.mean(xc * xc, axis=-1, keepdims=True) + EPS)
        g = g_ref[sl, :].astype(F32)
        o_ref[sl, :] = ((g * _sigmoid(g)) * y).astype(o_ref.dtype)


def retention(proj, cos, sin, log_gamma, batch):
    n = proj.shape[0]
    s = n // batch
    d = HEAD_DIM
    hh = N_RET_HEADS

    def col(off):
        return pl.BlockSpec((s, d), lambda b, h, lg: (b, off + h))

    tab = pl.BlockSpec((s, d), lambda b, h, lg: (0, 0))
    grid_spec = pltpu.PrefetchScalarGridSpec(
        num_scalar_prefetch=1,
        grid=(batch, hh),
        in_specs=[col(0), col(hh), col(2 * hh), col(3 * hh), tab, tab],
        out_specs=pl.BlockSpec((s, d), lambda b, h, lg: (b, h)),
    )
    return pl.pallas_call(
        _retention_kernel,
        grid_spec=grid_spec,
        out_shape=jax.ShapeDtypeStruct((n, hh * d), BF16),
        compiler_params=_cparams(2),
        name="retention",
    )(log_gamma, proj, proj, proj, proj, cos, sin)


def _fox_kernel(q_ref, k_ref, v_ref, ccol_ref, crow_ref, o_ref, *, tq):
    s, d = q_ref.shape
    h = pl.program_id(1)
    scale = d ** -0.5
    nt = (((1,), (1,)), ((), ()))
    lane = lax.broadcasted_iota(I32, (tq, LANES), 1)
    rr = lax.broadcasted_iota(I32, (tq, tq), 0)
    cc = lax.broadcasted_iota(I32, (tq, tq), 1)

    for qi in range(s // tq):
        qs = pl.ds(qi * tq, tq)
        q = q_ref[qs, :]
        cq = jnp.sum(jnp.where(lane == h, ccol_ref[qs, :], 0.0), axis=-1, keepdims=True)

        def scores(kb):
            ks = pl.ds(pl.multiple_of(kb * tq, tq), tq)
            sc = lax.dot_general(q, k_ref[ks, :], nt, preferred_element_type=F32) * scale
            return sc + cq - crow_ref[:, ks], v_ref[ks, :]

        def update(carry, sc, v):
            m, l, acc = carry
            m_new = jnp.maximum(m, jnp.max(sc, axis=-1, keepdims=True))
            alpha = jnp.exp(m - m_new)
            p = jnp.exp(sc - m_new)
            l = alpha * l + jnp.sum(p, axis=-1, keepdims=True)
            acc = alpha * acc + jnp.dot(p.astype(BF16), v, preferred_element_type=F32)
            return m_new, l, acc

        def body(kb, carry):
            sc, v = scores(kb)
            return update(carry, sc, v)

        carry = (jnp.full((tq, 1), NEG_BIG, F32), jnp.zeros((tq, 1), F32),
                 jnp.zeros((tq, d), F32))
        carry = lax.fori_loop(0, qi, body, carry, unroll=True)
        sc, v = scores(qi)
        _, l, acc = update(carry, jnp.where(rr >= cc, sc, NEG_BIG), v)
        o_ref[qs, :] = (acc / l).astype(o_ref.dtype)


def fox_attention(proj, c_cols, c_rows, batch, tq=256):
    n = proj.shape[0]
    s = n // batch
    d = HEAD_DIM
    base = 4 * N_RET_HEADS
    hh = N_FOX_HEADS

    def col(off):
        return pl.BlockSpec((s, d), lambda b, h: (b, off + h))

    return pl.pallas_call(
        functools.partial(_fox_kernel, tq=tq),
        grid=(batch, hh),
        in_specs=[col(base), col(base + hh), col(base + 2 * hh),
                  pl.BlockSpec((s, LANES), lambda b, h: (b, 0)),
                  pl.BlockSpec((None, None, 1, s), lambda b, h: (b, h, 0, 0))],
        out_specs=pl.BlockSpec((s, d), lambda b, h: (b, h)),
        out_shape=jax.ShapeDtypeStruct((n, hh * d), BF16),
        compiler_params=_cparams(2),
        name="fox_attention",
    )(proj, proj, proj, c_cols, c_rows)


def _pool_kernel(h_ref, o_ref, *, blocks_per_group):
    x = h_ref[...]
    row = lax.broadcasted_iota(I32, x.shape, 0)
    group = pl.program_id(1) // blocks_per_group

    def shifted(a, by):
        return jnp.where(row >= by, pltpu.roll(a, by, 0), 0.0)

    for gi, w in enumerate(POOL_WINDOWS):
        @pl.when(group == gi)
        def _():
            acc = x
            by = 1
            while by < w:
                acc = acc + shifted(acc, by)
                by *= 2
            count = jnp.minimum(row + 1, w).astype(F32)
            o_ref[...] = (acc / count - x).astype(o_ref.dtype)


def pool(h, batch, tc=256):
    n, d = h.shape
    s = n // batch
    group = d // len(POOL_WINDOWS)
    return pl.pallas_call(
        functools.partial(_pool_kernel, blocks_per_group=group // tc),
        grid=(batch, d // tc),
        in_specs=[pl.BlockSpec((s, tc), lambda b, j: (b, j))],
        out_specs=pl.BlockSpec((s, tc), lambda b, j: (b, j)),
        out_shape=jax.ShapeDtypeStruct((n, d), BF16),
        compiler_params=_cparams(2),
        name="pool",
    )(h)


def _route_kernel(lg_ref, ids_ref, gate_ref, cnt_ref, carry_ref):
    tt = lg_ref.shape[0]

    @pl.when(pl.program_id(0) == 0)
    def _():
        carry_ref[...] = jnp.zeros_like(carry_ref)

    lane = lax.broadcasted_iota(I32, (tt, LANES), 1)
    lane_f = lane.astype(F32)
    x = jnp.where(lane < N_EXPERTS, lg_ref[...], -jnp.inf)
    v1 = jnp.max(x, axis=-1, keepdims=True)
    i1 = jnp.min(jnp.where(x == v1, lane_f, float(LANES)), axis=-1, keepdims=True)
    x2 = jnp.where(lane_f == i1, -jnp.inf, x)
    v2 = jnp.max(x2, axis=-1, keepdims=True)
    i2 = jnp.min(jnp.where(x2 == v2, lane_f, float(LANES)), axis=-1, keepdims=True)
    e = jnp.exp(v2 - v1)
    w1 = 1.0 / (1.0 + e)
    w2 = e / (1.0 + e)
    sel1 = lane_f == i1
    sel2 = lane_f == i2
    onehot = jnp.where(sel1, 1.0, jnp.where(sel2, 1.0, 0.0))
    rr = lax.broadcasted_iota(I32, (tt, tt), 0)
    cc = lax.broadcasted_iota(I32, (tt, tt), 1)
    tri = jnp.where(rr > cc, 1.0, 0.0).astype(BF16)
    before = jnp.dot(tri, onehot.astype(BF16), preferred_element_type=F32) + carry_ref[0:1, :]
    r1 = jnp.sum(jnp.where(sel1, before, 0.0), axis=-1, keepdims=True)
    r2 = jnp.sum(jnp.where(sel2, before, 0.0), axis=-1, keepdims=True)
    ids = jnp.where(lane == 0, i1, jnp.where(lane == 1, i2,
                    jnp.where(lane == 2, r1, jnp.where(lane == 3, r2, 0.0))))
    ids_ref[...] = ids.astype(I32)
    gate_ref[...] = jnp.where(lane == 0, w1, jnp.where(lane == 1, w2, 0.0))
    total = carry_ref[0:1, :] + jnp.sum(onehot, axis=0, keepdims=True)
    carry_ref[...] = jnp.broadcast_to(total, carry_ref.shape)
    cnt_ref[...] = jnp.broadcast_to(total, cnt_ref.shape).astype(I32)


def route(logits, tt=256):
    n = logits.shape[0]
    row = pl.BlockSpec((tt, LANES), lambda i: (i, 0))
    return pl.pallas_call(
        _route_kernel,
        grid=(n // tt,),
        in_specs=[row],
        out_specs=[row, row, pl.BlockSpec((8, LANES), lambda i: (0, 0))],
        out_shape=[jax.ShapeDtypeStruct((n, LANES), I32),
                   jax.ShapeDtypeStruct((n, LANES), F32),
                   jax.ShapeDtypeStruct((8, LANES), I32)],
        scratch_shapes=[pltpu.VMEM((8, LANES), F32)],
        compiler_params=_cparams(1),
        name="route",
    )(logits)


def _gather_kernel(pos1_ref, pos2_ref, h_hbm, o_ref, src_ref, buf_ref, sems, *,
                   n_tokens, nb):
    tg = o_ref.shape[0]
    i = pl.program_id(0)
    n_tiles = pl.num_programs(0) - 1

    @pl.when(i == 0)
    def _():
        def clear(p, carry):
            src_ref[p] = 0
            return carry
        lax.fori_loop(0, src_ref.shape[0], clear, 0, unroll=8)

        def place(t, carry):
            src_ref[pos1_ref[t]] = t
            src_ref[pos2_ref[t]] = t
            return carry
        lax.fori_loop(0, n_tokens, place, 0, unroll=4)

    def slab_copy(tile, slot, r):
        src = pl.multiple_of(src_ref[tile * tg + r] * nb, nb)
        dst = pl.multiple_of(r * nb, nb)
        return pltpu.make_async_copy(
            h_hbm.at[pl.ds(src, nb), :], buf_ref.at[slot, pl.ds(dst, nb), :], sems.at[slot])

    @pl.when(i < n_tiles)
    def _():
        def issue(r, carry):
            slab_copy(i, i % 2, r).start()
            return carry
        lax.fori_loop(0, tg, issue, 0, unroll=4)

    @pl.when(i > 0)
    def _():
        slot = (i - 1) % 2

        def drain(r, carry):
            slab_copy(i - 1, slot, r).wait()
            return carry
        lax.fori_loop(0, tg, drain, 0, unroll=4)
        half = nb * LANES
        for c in range(nb):
            lo, hi = _unpack_bf16_pairs(buf_ref[slot, pl.ds(c, tg, stride=nb), :])
            o_ref[:, c * LANES:(c + 1) * LANES] = lo
            o_ref[:, half + c * LANES:half + (c + 1) * LANES] = hi


def gather_rows(h_packed, pos1, pos2, n_rows, d, tg=256):
    nb = d // 2 // LANES
    n = h_packed.shape[0] // nb
    width = 2 * nb * LANES
    n_tiles = n_rows // tg
    grid_spec = pltpu.PrefetchScalarGridSpec(
        num_scalar_prefetch=2,
        grid=(n_tiles + 1,),
        in_specs=[pl.BlockSpec(memory_space=pl.ANY)],
        out_specs=pl.BlockSpec((tg, width), lambda i, p1, p2: (jnp.maximum(i - 1, 0), 0)),
        scratch_shapes=[pltpu.SMEM((n_rows,), I32), pltpu.VMEM((2, tg * nb, LANES), U32),
                        pltpu.SemaphoreType.DMA((2,))],
    )
    return pl.pallas_call(
        functools.partial(_gather_kernel, n_tokens=n, nb=nb),
        grid_spec=grid_spec,
        out_shape=jax.ShapeDtypeStruct((n_rows, width), BF16),
        compiler_params=_cparams(1),
        name="gather_rows",
    )(pos1, pos2, h_packed)


def _combine_kernel(pos1_ref, pos2_ref, o_hbm, gate_ref, x_ref, g_ref, out_ref, buf_ref, sems):
    tt = x_ref.shape[0]
    i = pl.program_id(0)
    n_tiles = pl.num_programs(0) - 1

    def row_copy(tile, slot, r, which, pos_ref):
        return pltpu.make_async_copy(
            o_hbm.at[pl.ds(pos_ref[tile * tt + r], 1), :],
            buf_ref.at[slot, which, pl.ds(r, 1), :], sems.at[slot])

    @pl.when(i < n_tiles)
    def _():
        def issue(r, carry):
            row_copy(i, i % 2, r, 0, pos1_ref).start()
            row_copy(i, i % 2, r, 1, pos2_ref).start()
            return carry
        lax.fori_loop(0, tt, issue, 0, unroll=4)

    @pl.when(i > 0)
    def _():
        slot = (i - 1) % 2

        def drain(r, carry):
            row_copy(i - 1, slot, r, 0, pos1_ref).wait()
            row_copy(i - 1, slot, r, 1, pos2_ref).wait()
            return carry
        lax.fori_loop(0, tt, drain, 0, unroll=4)
        gate = gate_ref[...]
        y = gate[:, 0:1] * buf_ref[slot, 0] + gate[:, 1:2] * buf_ref[slot, 1]
        out_ref[...] = x_ref[...] + (y * _rms(y)) * g_ref[...]


def combine(o, pos1, pos2, gate, x, g, tt=128):
    n, d = x.shape
    row = lambda i, p1, p2: (jnp.maximum(i - 1, 0), 0)
    grid_spec = pltpu.PrefetchScalarGridSpec(
        num_scalar_prefetch=2,
        grid=(n // tt + 1,),
        in_specs=[pl.BlockSpec(memory_space=pl.ANY),
                  pl.BlockSpec((tt, LANES), row),
                  pl.BlockSpec((tt, d), row),
                  pl.BlockSpec((1, d), lambda i, p1, p2: (0, 0))],
        out_specs=pl.BlockSpec((tt, d), row),
        scratch_shapes=[pltpu.VMEM((2, 2, tt, d), F32), pltpu.SemaphoreType.DMA((2,))],
    )
    return pl.pallas_call(
        _combine_kernel,
        grid_spec=grid_spec,
        out_shape=jax.ShapeDtypeStruct((n, d), F32),
        compiler_params=_cparams(1),
        name="combine",
    )(pos1, pos2, o, gate, x, g.reshape(1, d))


def _rope_tables(s):
    half = HEAD_DIM // 2
    inv = ROPE_BASE ** (-jnp.arange(half, dtype=F32) / half)
    ang = jnp.arange(s, dtype=F32)[:, None] * inv[None, :]
    cos, sin = jnp.cos(ang), jnp.sin(ang)
    return jnp.concatenate([cos, cos], axis=-1), jnp.concatenate([-sin, sin], axis=-1)


def _moe_schedule(ids, counts, tm, n_tiles, nc):
    ne = N_EXPERTS
    counts = counts[0, :ne]
    tiles = (counts + tm - 1) // tm
    nonempty = tiles > 0
    tile_end = jnp.cumsum(tiles)
    tile_start = tile_end - tiles
    pos1 = tile_start[ids[:, 0]] * tm + ids[:, 2]
    pos2 = tile_start[ids[:, 1]] * tm + ids[:, 3]

    steps = jnp.where(nonempty, jnp.maximum(tiles, nc), 0)
    step_end = jnp.cumsum(steps)
    step_start = step_end - steps
    total, n_used = step_end[-1], tile_end[-1]
    ar = jnp.arange(ne, dtype=I32)
    first_e = jnp.min(jnp.where(nonempty, ar, ne))
    last_e = jnp.max(jnp.where(nonempty, ar, -1))
    later = (ar[None, :] > ar[:, None]) & nonempty[None, :]
    nxt = jnp.min(jnp.where(later, ar[None, :], ne), axis=1)
    wrap = nxt == ne
    nxt = jnp.where(wrap, first_e, nxt)
    rank = jnp.cumsum(nonempty.astype(I32)) - 1

    s = jnp.arange(n_tiles + ne * (nc - 1), dtype=I32)
    in_blocks = s < total
    e = jnp.minimum(jnp.sum((s[:, None] >= step_end[None, :]).astype(I32), axis=1), ne - 1)
    e = jnp.where(in_blocks, e, last_e)
    p = s - step_start[e]
    is_comp = in_blocks & (p < tiles[e])
    z = s - total
    is_zero = (~in_blocks) & (z < n_tiles - n_used)
    group_last = tile_start[e] + tiles[e] - 1
    tile = jnp.where(is_comp, tile_start[e] + p,
                     jnp.where(is_zero, n_used + z, jnp.where(in_blocks, group_last, n_tiles - 1)))
    arow = jnp.where(is_comp, tile, jnp.where(in_blocks, group_last, n_used - 1))
    kind = jnp.where(is_comp, KIND_COMPUTE, jnp.where(is_zero, KIND_ZERO, KIND_NONE))
    pf_chunk = jnp.where(in_blocks, jnp.minimum(p, nc - 1), nc - 1)
    pf_do = in_blocks & (p < nc)
    sched = (tile, arow, kind, rank[e], nxt[e], wrap[e], pf_chunk, pf_do,
             jnp.sum(nonempty.astype(I32)).reshape(1))
    return pos1.astype(I32), pos2.astype(I32), tuple(v.astype(I32) for v in sched)


def kernel(x, even_norm_mix_pre, even_w_in, even_b_forget, even_w_out, even_norm_mix_post,
           even_norm_ffn_pre, even_w_gate, even_w_up, even_w_down, even_norm_ffn_post,
           odd_norm_mix_pre, odd_w_pool, odd_pool_scale, odd_norm_mix_post, odd_norm_ffn_pre,
           odd_w_router, odd_we_gate, odd_we_up, odd_we_down, odd_norm_ffn_post):
    batch, seq, d = x.shape
    n = batch * seq
    x0 = x.reshape(n, d)
    ret_w = N_RET_HEADS * HEAD_DIM
    fox_w = N_FOX_HEADS * HEAD_DIM
    main_cols = 4 * ret_w + 3 * fox_w

    h0 = rms_cast(x0, even_norm_mix_pre[0])
    w_in_t = jnp.swapaxes(even_w_in, 1, 2)
    proj = gmm(h0, [w_in_t], tm=1024, tn=1024, n_out=main_cols, out_dtype=BF16, nc=4,
               transposed=True)
    b_forget = jnp.pad(even_b_forget[0], (0, LANES - N_FOX_HEADS)).reshape(1, LANES)
    f_logit = gmm(h0, [w_in_t], tm=512, tn=LANES, n_out=LANES, out_dtype=F32, nc=1,
                  transposed=True, w_col=lambda j: main_cols // LANES + j,
                  n_valid=N_FOX_HEADS)
    c_cols = forget_cumsum(f_logit, b_forget, batch)
    c_rows = c_cols[:, :N_FOX_HEADS].reshape(batch, seq, N_FOX_HEADS)
    c_rows = c_rows.transpose(0, 2, 1).reshape(batch, N_FOX_HEADS, 1, seq)
    cos, sin = _rope_tables(seq)
    log_gamma = jnp.log1p(-(2.0 ** (-5.0 - jnp.arange(N_RET_HEADS, dtype=F32))))
    ret = retention(proj, cos, sin, log_gamma, batch)
    fox = fox_attention(proj, c_cols, c_rows, batch)
    m = gmm([ret, fox], [even_w_out], tm=1024, tn=1024, n_out=d, out_dtype=F32, nc=8)
    x1, h1 = resid_norm(x0, m, even_norm_mix_post[0], even_norm_ffn_pre[0], "bf16")

    d_ff = even_w_gate.shape[-1]
    act = gmm(h1, [even_w_gate, even_w_up], tm=1024, tn=512, n_out=d_ff, out_dtype=BF16, nc=4)
    f = gmm(act, [even_w_down], tm=256, tn=512, n_out=d, out_dtype=F32, nc=8)
    x2, h2 = resid_norm(x1, f, even_norm_ffn_post[0], odd_norm_mix_pre[0], "f32")

    pooled = pool(h2, batch)
    group = d // len(POOL_WINDOWS)
    tn_pool = 512
    per = group // tn_pool
    m2 = gmm(pooled, [odd_w_pool[0]], tm=2048, tn=tn_pool, n_out=d, out_dtype=F32, nc=2,
             k=group, scale=odd_pool_scale[0].reshape(1, d),
             a_col=lambda j: j // per, w_group=lambda j: j // per, w_col=lambda j: j % per)
    w_router = jnp.pad(odd_w_router[0], ((0, 0), (0, LANES - N_EXPERTS)))
    x3, h3, logits = resid_norm(x2, m2, odd_norm_mix_post[0], odd_norm_ffn_pre[0], "packed",
                                w_router=w_router)

    ids, gate, counts = route(logits)
    tm = MOE_TILE
    n_rows = 2 * n + N_EXPERTS * tm
    pos1, pos2, sched = _moe_schedule(ids, counts, tm, n_rows // tm, MOE_CHUNKS)
    xs = gather_rows(h3, pos1, pos2, n_rows, d)
    d_fe = odd_we_gate.shape[-1]
    act2 = gmm(xs, [odd_we_gate[0], odd_we_up[0]], tm=tm, tn=512, n_out=d_fe,
               out_dtype=BF16, nc=MOE_CHUNKS, sched=sched)
    o = gmm(act2, [odd_we_down[0]], tm=tm, tn=512, n_out=d, out_dtype=F32,
            nc=MOE_CHUNKS, sched=sched)
    out = combine(o, pos1, pos2, gate, x3, odd_norm_ffn_post[0])
    return out.reshape(batch, seq, d)
```

```python
import functools

import jax
import jax.numpy as jnp
from jax import lax
from jax.experimental import pallas as pl
from jax.experimental.pallas import tpu as pltpu

F32 = jnp.float32
BF16 = jnp.bfloat16
U32 = jnp.uint32
I32 = jnp.int32

HEAD_DIM = 128
N_RET_HEADS = 16
N_FOX_HEADS = 16
RET_CHUNK = 128
ROPE_BASE = 10000.0
N_EXPERTS = 8
POOL_WINDOWS = (2, 4, 8, 16)
EPS = 1e-6
LANES = 128
NEG_BIG = -1e30

VMEM_LIMIT = 56 * 1024 * 1024
MOE_TILE = 512
MOE_CHUNKS = 2


def _cparams(n_axes):
    return pltpu.CompilerParams(
        dimension_semantics=("arbitrary",) * n_axes, vmem_limit_bytes=VMEM_LIMIT)


def _rms(v):
    return lax.rsqrt(jnp.mean(v * v, axis=-1, keepdims=True) + EPS)


def _sigmoid(v):
    return 1.0 / (1.0 + jnp.exp(-v))


def _rms_cast_kernel(x_ref, g_ref, o_ref):
    x = x_ref[...]
    o_ref[...] = ((x * _rms(x)) * g_ref[...]).astype(o_ref.dtype)


def rms_cast(x, g, tm=256):
    n, d = x.shape
    return pl.pallas_call(
        _rms_cast_kernel,
        grid=(n // tm,),
        in_specs=[pl.BlockSpec((tm, d), lambda i: (i, 0)),
                  pl.BlockSpec((1, d), lambda i: (0, 0))],
        out_specs=pl.BlockSpec((tm, d), lambda i: (i, 0)),
        out_shape=jax.ShapeDtypeStruct((n, d), BF16),
        compiler_params=_cparams(1),
        name="rms_cast",
    )(x, g.reshape(1, d))


def _pack_bf16_pairs(h):
    half = h.shape[1] // 2
    bits = lax.bitcast_convert_type(h.astype(BF16).astype(F32), U32)
    return (bits[:, :half] >> 16) | (bits[:, half:] & jnp.uint32(0xFFFF0000))


def _unpack_bf16_pairs(p):
    lo = lax.bitcast_convert_type(p << 16, F32).astype(BF16)
    hi = lax.bitcast_convert_type(p & jnp.uint32(0xFFFF0000), F32).astype(BF16)
    return lo, hi


def _resid_norm_kernel(*refs, h_mode, router):
    x_ref, m_ref, gp_ref = refs[:3]
    pos = 3
    gn_ref = wr_ref = None
    if h_mode is not None:
        gn_ref = refs[pos]; pos += 1
    if router:
        wr_ref = refs[pos]; pos += 1
    xo_ref = refs[pos]; pos += 1
    m = m_ref[...]
    xn = x_ref[...] + (m * _rms(m)) * gp_ref[...]
    xo_ref[...] = xn
    if h_mode is None:
        return
    h = (xn * _rms(xn)) * gn_ref[...]
    h_ref = refs[pos]; pos += 1
    if h_mode == "packed":
        packed = _pack_bf16_pairs(h)
        nb = packed.shape[1] // LANES
        for c in range(nb):
            h_ref[pl.ds(c, packed.shape[0], stride=nb), :] = packed[:, c * LANES:(c + 1) * LANES]
    else:
        h_ref[...] = h.astype(h_ref.dtype)
    if router:
        refs[pos][...] = jnp.dot(h, wr_ref[...], preferred_element_type=F32,
                                 precision=lax.Precision.HIGHEST)


def resid_norm(x, m, g_post, g_next=None, h_mode=None, w_router=None, tm=256):
    n, d = x.shape
    row = pl.BlockSpec((tm, d), lambda i: (i, 0))
    vec = pl.BlockSpec((1, d), lambda i: (0, 0))
    args = [x, m, g_post.reshape(1, d)]
    in_specs = [row, row, vec]
    out_shape = [jax.ShapeDtypeStruct((n, d), F32)]
    out_specs = [row]
    if h_mode is not None:
        args.append(g_next.reshape(1, d)); in_specs.append(vec)
        if h_mode == "packed":
            nb = d // 2 // LANES
            out_shape.append(jax.ShapeDtypeStruct((n * nb, LANES), U32))
            out_specs.append(pl.BlockSpec((tm * nb, LANES), lambda i: (i, 0)))
        else:
            out_shape.append(jax.ShapeDtypeStruct((n, d), BF16 if h_mode == "bf16" else F32))
            out_specs.append(row)
    router = w_router is not None
    if router:
        args.append(w_router)
        in_specs.append(pl.BlockSpec((d, LANES), lambda i: (0, 0)))
        out_shape.append(jax.ShapeDtypeStruct((n, LANES), F32))
        out_specs.append(pl.BlockSpec((tm, LANES), lambda i: (i, 0)))
    return pl.pallas_call(
        functools.partial(_resid_norm_kernel, h_mode=h_mode, router=router),
        grid=(n // tm,),
        in_specs=in_specs, out_specs=out_specs, out_shape=out_shape,
        compiler_params=_cparams(1),
        name="resid_norm",
    )(*args)


N_SCHED = 9
KIND_NONE, KIND_COMPUTE, KIND_ZERO, KIND_HALF = 0, 1, 2, 3


def _gmm_kernel(tile_ref, arow_ref, kind_ref, blk_ref, pfg_ref, pfw_ref, pfc_ref, pfd_ref,
                nblk_ref, *refs, n_a, n_w, n_col, has_scale, transposed, n_valid, half_tiles):
    del tile_ref, arow_ref, pfg_ref
    a_refs = refs[:n_a]
    c_refs = refs[n_a:n_a + n_w]
    pos = n_a + n_w
    s_ref = None
    if has_scale:
        s_ref = refs[pos]; pos += 1
    o_ref = refs[pos]; pos += 1
    slots = (refs[pos:pos + n_w], refs[pos + n_w:pos + 2 * n_w])
    jj = pl.program_id(0)
    s = pl.program_id(1)
    parity = (jj * nblk_ref[0] + blk_ref[s] + nblk_ref[0]) % 2
    rows_per_chunk = c_refs[0].shape[1]
    tcol = jj - 1 + pfw_ref[s]
    chunk_due = (pfd_ref[s] == 1) & (tcol >= 0) & (tcol < n_col)
    kind = kind_ref[s]
    computing = ((kind == KIND_COMPUTE) | (kind == KIND_HALF)) & (jj >= 1)
    tm = o_ref.shape[0]

    def cast_chunk(nxt):
        r = pl.multiple_of(pfc_ref[s] * rows_per_chunk, rows_per_chunk)
        for c_ref, wbf in zip(c_refs, nxt):
            wbf[pl.ds(r, rows_per_chunk), :] = c_ref[0].astype(BF16)

    def matmul(wbf, rows):
        if transposed:
            return lax.dot_general(a_refs[0][0:rows, :], wbf[...], (((1,), (1,)), ((), ())),
                                   preferred_element_type=F32)
        acc = None
        off = 0
        for a_ref in a_refs:
            ka = a_ref.shape[1]
            part = jnp.dot(a_ref[0:rows, :], wbf[off:off + ka, :], preferred_element_type=F32)
            acc = part if acc is None else acc + part
            off += ka
        return acc

    def compute(cur, nxt, rows):
        if merge_cast:
            cast_chunk(nxt)
        acc = matmul(cur[0], rows)
        if n_w == 2:
            acc = (acc * _sigmoid(acc)) * matmul(cur[1], rows)
        if has_scale:
            acc = acc * s_ref[...]
        if n_valid is not None:
            col = lax.broadcasted_iota(I32, acc.shape, 1) + (jj - 1) * acc.shape[1]
            acc = jnp.where(col < n_valid, acc, 0.0)
        o_ref[0:rows, :] = acc.astype(o_ref.dtype)
        if rows < tm:
            o_ref[rows:tm, :] = jnp.zeros((tm - rows, o_ref.shape[1]), o_ref.dtype)

    heights = ((KIND_COMPUTE, tm), (KIND_HALF, tm // 2)) if half_tiles else ((KIND_COMPUTE, tm),)
    merge_cast = not half_tiles
    for p in (0, 1):
        cur, nxt = slots[p], slots[1 - p]
        for which, rows in heights:
            pl.when((kind == which) & (jj >= 1) & (parity == p))(
                functools.partial(compute, cur, nxt, rows))
        cast_here = chunk_due & jnp.logical_not(computing) if merge_cast else chunk_due
        pl.when(cast_here & (parity == p))(functools.partial(cast_chunk, nxt))

    @pl.when((kind_ref[s] == KIND_ZERO) & (jj >= 1))
    def _():
        o_ref[...] = jnp.zeros_like(o_ref)


def _dense_schedule(n_tiles, nc):
    t = jnp.arange(n_tiles, dtype=I32)
    zeros = jnp.zeros((n_tiles,), I32)
    ones = jnp.ones((n_tiles,), I32)
    return (t, t, ones, zeros, zeros, ones, jnp.minimum(t, nc - 1), (t < nc).astype(I32),
            jnp.ones((1,), I32))


def gmm(a, ws, *, tm, tn, n_out, out_dtype, nc, sched=None, k=None, scale=None,
        a_col=None, w_group=None, w_col=None, transposed=False, n_valid=None):
    a_parts = list(a) if isinstance(a, (list, tuple)) else [a]
    rows = a_parts[0].shape[0]
    if k is None:
        k = sum(p.shape[1] for p in a_parts)
    sched_given = sched is not None
    if not sched_given:
        sched = _dense_schedule(rows // tm, nc)
    n_steps = sched[0].shape[0]
    n_col = pl.cdiv(n_out, tn)
    a_col = a_col or (lambda j: 0)
    w_col = w_col or (lambda j: j)

    def cur_col(jj):
        return jnp.maximum(jj - 1, 0)

    def a_map(jj, s, tile, arow, *_):
        return (jnp.where(jj == 0, arow[0], arow[s]), a_col(cur_col(jj)))

    def o_map(jj, s, tile, *_):
        return (jnp.where(jj == 0, tile[0], tile[s]), cur_col(jj))

    def c_map(jj, s, tile, arow, kind, blk, pfg, pfw, pfc, *_):
        col = jnp.clip(jj - 1 + pfw[s], 0, n_col - 1)
        group = pfg[s] if w_group is None else w_group(col)
        if transposed:
            return (group, w_col(col) * nc + pfc[s], 0)
        return (group, pfc[s], w_col(col))

    in_specs = [pl.BlockSpec((tm, k if len(a_parts) == 1 else p.shape[1]), a_map)
                for p in a_parts]
    if transposed:
        assert len(a_parts) == 1 and tn % nc == 0
        chunk, wbf_shape = (1, tn // nc, k), (tn, k)
    else:
        assert k % nc == 0
        chunk, wbf_shape = (1, k // nc, tn), (k, tn)
    in_specs += [pl.BlockSpec(chunk, c_map) for _ in ws]
    args = [*a_parts, *ws]
    if scale is not None:
        in_specs.append(pl.BlockSpec((1, tn), lambda jj, s, *_: (0, cur_col(jj))))
        args.append(scale)
    grid_spec = pltpu.PrefetchScalarGridSpec(
        num_scalar_prefetch=N_SCHED,
        grid=(n_col + 1, n_steps),
        in_specs=in_specs,
        out_specs=pl.BlockSpec((tm, tn), o_map),
        scratch_shapes=[pltpu.VMEM(wbf_shape, BF16) for _ in range(2 * len(ws))],
    )
    return pl.pallas_call(
        functools.partial(_gmm_kernel, n_a=len(a_parts), n_w=len(ws), n_col=n_col,
                          has_scale=scale is not None, transposed=transposed,
                          n_valid=n_valid, half_tiles=sched_given),
        grid_spec=grid_spec,
        out_shape=jax.ShapeDtypeStruct((rows, n_out), out_dtype),
        compiler_params=_cparams(2),
        name="gmm",
    )(*sched, *args)


def _forget_cumsum_kernel(f_ref, b_ref, o_ref):
    x = f_ref[...] + b_ref[...]
    ls = jnp.minimum(x, 0.0) - jnp.log1p(jnp.exp(-jnp.abs(x)))
    s = ls.shape[0]
    row = lax.broadcasted_iota(I32, ls.shape, 0)
    sh = 1
    while sh < s:
        ls = ls + jnp.where(row >= sh, pltpu.roll(ls, sh, 0), 0.0)
        sh *= 2
    o_ref[...] = ls


def forget_cumsum(f, b, batch):
    n, w = f.shape
    s = n // batch
    return pl.pallas_call(
        _forget_cumsum_kernel,
        grid=(batch,),
        in_specs=[pl.BlockSpec((s, w), lambda i: (i, 0)),
                  pl.BlockSpec((1, w), lambda i: (0, 0))],
        out_specs=pl.BlockSpec((s, w), lambda i: (i, 0)),
        out_shape=jax.ShapeDtypeStruct((n, w), F32),
        compiler_params=_cparams(1),
        name="forget_cumsum",
    )(f, b)


def _retention_kernel(lg_ref, q_ref, k_ref, v_ref, g_ref, cos_ref, sin_ref, o_ref):
    c = RET_CHUNK
    d = HEAD_DIM
    s = q_ref.shape[0]
    lg = lg_ref[pl.program_id(1)]
    ii = lax.broadcasted_iota(I32, (c, c), 0)
    jj = lax.broadcasted_iota(I32, (c, c), 1)
    diff = (ii - jj).astype(F32)
    decay_in = jnp.where(diff >= 0, jnp.exp(lg * jnp.maximum(diff, 0.0)), 0.0)
    jc = lax.broadcasted_iota(I32, (c, 1), 0).astype(F32)
    zeta = jnp.exp(lg * (c - 1 - jc))
    q_decay = jnp.exp(lg * (jc + 1.0))
    gamma_c = jnp.exp(jnp.full((1, 1), lg, F32) * c)
    scale = d ** -0.5
    nt = (((1,), (1,)), ((), ()))
    tn = (((0,), (0,)), ((), ()))
    state = jnp.zeros((d, d), F32)
    for n in range(s // c):
        sl = pl.ds(n * c, c)
        cos = cos_ref[sl, :]
        sin = sin_ref[sl, :]
        q = q_ref[sl, :].astype(F32)
        k = k_ref[sl, :].astype(F32)
        v = v_ref[sl, :]
        qr = q * cos + pltpu.roll(q, d // 2, 1) * sin
        kr = (k * cos + pltpu.roll(k, d // 2, 1) * sin) * scale
        scores = lax.dot_general(qr.astype(BF16), kr.astype(BF16), nt,
                                 preferred_element_type=F32) * decay_in
        inner = jnp.dot(scores.astype(BF16), v, preferred_element_type=F32)
        cross = jnp.dot((qr * q_decay).astype(BF16), state.astype(BF16),
                        preferred_element_type=F32)
        kv = lax.dot_general((kr * zeta).astype(BF16), v, tn, preferred_element_type=F32)
        state = gamma_c * state + kv
        o = inner + cross
        xc = o - jnp.mean(o, axis=-1, keepdims=True)
        y = xc * lax.rsqrt(jnp.mean(xc * xc, axis=-1, keepdims=True) + EPS)
        g = g_ref[sl, :].astype(F32)
        o_ref[sl, :] = ((g * _sigmoid(g)) * y).astype(o_ref.dtype)


def retention(proj, cos, sin, log_gamma, batch):
    n = proj.shape[0]
    s = n // batch
    d = HEAD_DIM
    hh = N_RET_HEADS

    def col(off):
        return pl.BlockSpec((s, d), lambda b, h, lg: (b, off + h))

    tab = pl.BlockSpec((s, d), lambda b, h, lg: (0, 0))
    grid_spec = pltpu.PrefetchScalarGridSpec(
        num_scalar_prefetch=1,
        grid=(batch, hh),
        in_specs=[col(0), col(hh), col(2 * hh), col(3 * hh), tab, tab],
        out_specs=pl.BlockSpec((s, d), lambda b, h, lg: (b, h)),
    )
    return pl.pallas_call(
        _retention_kernel,
        grid_spec=grid_spec,
        out_shape=jax.ShapeDtypeStruct((n, hh * d), BF16),
        compiler_params=_cparams(2),
        name="retention",
    )(log_gamma, proj, proj, proj, proj, cos, sin)


def _fox_kernel(q_ref, k_ref, v_ref, ccol_ref, crow_ref, o_ref, *, tq):
    s, d = q_ref.shape
    h = pl.program_id(1)
    scale = d ** -0.5
    nt = (((1,), (1,)), ((), ()))
    lane = lax.broadcasted_iota(I32, (tq, LANES), 1)
    rr = lax.broadcasted_iota(I32, (tq, tq), 0)
    cc = lax.broadcasted_iota(I32, (tq, tq), 1)

    for qi in range(s // tq):
        qs = pl.ds(qi * tq, tq)
        q = q_ref[qs, :]
        cq = jnp.sum(jnp.where(lane == h, ccol_ref[qs, :], 0.0), axis=-1, keepdims=True)

        def scores(kb):
            ks = pl.ds(pl.multiple_of(kb * tq, tq), tq)
            sc = lax.dot_general(q, k_ref[ks, :], nt, preferred_element_type=F32) * scale
            return sc + cq - crow_ref[:, ks], v_ref[ks, :]

        def update(carry, sc, v):
            m, l, acc = carry
            m_new = jnp.maximum(m, jnp.max(sc, axis=-1, keepdims=True))
            alpha = jnp.exp(m - m_new)
            p = jnp.exp(sc - m_new)
            l = alpha * l + jnp.sum(p, axis=-1, keepdims=True)
            acc = alpha * acc + jnp.dot(p.astype(BF16), v, preferred_element_type=F32)
            return m_new, l, acc

        def body(kb, carry):
            sc, v = scores(kb)
            return update(carry, sc, v)

        carry = (jnp.full((tq, 1), NEG_BIG, F32), jnp.zeros((tq, 1), F32),
                 jnp.zeros((tq, d), F32))
        carry = lax.fori_loop(0, qi, body, carry, unroll=True)
        sc, v = scores(qi)
        _, l, acc = update(carry, jnp.where(rr >= cc, sc, NEG_BIG), v)
        o_ref[qs, :] = (acc / l).astype(o_ref.dtype)


def fox_attention(proj, c_cols, c_rows, batch, tq=256):
    n = proj.shape[0]
    s = n // batch
    d = HEAD_DIM
    base = 4 * N_RET_HEADS
    hh = N_FOX_HEADS

    def col(off):
        return pl.BlockSpec((s, d), lambda b, h: (b, off + h))

    return pl.pallas_call(
        functools.partial(_fox_kernel, tq=tq),
        grid=(batch, hh),
        in_specs=[col(base), col(base + hh), col(base + 2 * hh),
                  pl.BlockSpec((s, LANES), lambda b, h: (b, 0)),
                  pl.BlockSpec((None, None, 1, s), lambda b, h: (b, h, 0, 0))],
        out_specs=pl.BlockSpec((s, d), lambda b, h: (b, h)),
        out_shape=jax.ShapeDtypeStruct((n, hh * d), BF16),
        compiler_params=_cparams(2),
        name="fox_attention",
    )(proj, proj, proj, c_cols, c_rows)


def _pool_kernel(h_ref, o_ref, *, blocks_per_group):
    x = h_ref[...]
    row = lax.broadcasted_iota(I32, x.shape, 0)
    group = pl.program_id(1) // blocks_per_group

    def shifted(a, by):
        return jnp.where(row >= by, pltpu.roll(a, by, 0), 0.0)

    for gi, w in enumerate(POOL_WINDOWS):
        @pl.when(group == gi)
        def _():
            acc = x
            by = 1
            while by < w:
                acc = acc + shifted(acc, by)
                by *= 2
            count = jnp.minimum(row + 1, w).astype(F32)
            o_ref[...] = (acc / count - x).astype(o_ref.dtype)


def pool(h, batch, tc=256):
    n, d = h.shape
    s = n // batch
    group = d // len(POOL_WINDOWS)
    return pl.pallas_call(
        functools.partial(_pool_kernel, blocks_per_group=group // tc),
        grid=(batch, d // tc),
        in_specs=[pl.BlockSpec((s, tc), lambda b, j: (b, j))],
        out_specs=pl.BlockSpec((s, tc), lambda b, j: (b, j)),
        out_shape=jax.ShapeDtypeStruct((n, d), BF16),
        compiler_params=_cparams(2),
        name="pool",
    )(h)


def _route_kernel(lg_ref, ids_ref, gate_ref, cnt_ref, carry_ref):
    tt = lg_ref.shape[0]

    @pl.when(pl.program_id(0) == 0)
    def _():
        carry_ref[...] = jnp.zeros_like(carry_ref)

    lane = lax.broadcasted_iota(I32, (tt, LANES), 1)
    lane_f = lane.astype(F32)
    x = jnp.where(lane < N_EXPERTS, lg_ref[...], -jnp.inf)
    v1 = jnp.max(x, axis=-1, keepdims=True)
    i1 = jnp.min(jnp.where(x == v1, lane_f, float(LANES)), axis=-1, keepdims=True)
    x2 = jnp.where(lane_f == i1, -jnp.inf, x)
    v2 = jnp.max(x2, axis=-1, keepdims=True)
    i2 = jnp.min(jnp.where(x2 == v2, lane_f, float(LANES)), axis=-1, keepdims=True)
    e = jnp.exp(v2 - v1)
    w1 = 1.0 / (1.0 + e)
    w2 = e / (1.0 + e)
    sel1 = lane_f == i1
    sel2 = lane_f == i2
    onehot = jnp.where(sel1, 1.0, jnp.where(sel2, 1.0, 0.0))
    rr = lax.broadcasted_iota(I32, (tt, tt), 0)
    cc = lax.broadcasted_iota(I32, (tt, tt), 1)
    tri = jnp.where(rr > cc, 1.0, 0.0).astype(BF16)
    before = jnp.dot(tri, onehot.astype(BF16), preferred_element_type=F32) + carry_ref[0:1, :]
    r1 = jnp.sum(jnp.where(sel1, before, 0.0), axis=-1, keepdims=True)
    r2 = jnp.sum(jnp.where(sel2, before, 0.0), axis=-1, keepdims=True)
    ids = jnp.where(lane == 0, i1, jnp.where(lane == 1, i2,
                    jnp.where(lane == 2, r1, jnp.where(lane == 3, r2, 0.0))))
    ids_ref[...] = ids.astype(I32)
    gate_ref[...] = jnp.where(lane == 0, w1, jnp.where(lane == 1, w2, 0.0))
    total = carry_ref[0:1, :] + jnp.sum(onehot, axis=0, keepdims=True)
    carry_ref[...] = jnp.broadcast_to(total, carry_ref.shape)
    cnt_ref[...] = jnp.broadcast_to(total, cnt_ref.shape).astype(I32)


def route(logits, tt=256):
    n = logits.shape[0]
    row = pl.BlockSpec((tt, LANES), lambda i: (i, 0))
    return pl.pallas_call(
        _route_kernel,
        grid=(n // tt,),
        in_specs=[row],
        out_specs=[row, row, pl.BlockSpec((8, LANES), lambda i: (0, 0))],
        out_shape=[jax.ShapeDtypeStruct((n, LANES), I32),
                   jax.ShapeDtypeStruct((n, LANES), F32),
                   jax.ShapeDtypeStruct((8, LANES), I32)],
        scratch_shapes=[pltpu.VMEM((8, LANES), F32)],
        compiler_params=_cparams(1),
        name="route",
    )(logits)


def _gather_kernel(pos1_ref, pos2_ref, h_hbm, o_ref, src_ref, buf_ref, sems, *,
                   n_tokens, nb):
    tg = o_ref.shape[0]
    i = pl.program_id(0)
    n_tiles = pl.num_programs(0) - 1

    @pl.when(i == 0)
    def _():
        def clear(p, carry):
            src_ref[p] = 0
            return carry
        lax.fori_loop(0, src_ref.shape[0], clear, 0, unroll=8)

        def place(t, carry):
            src_ref[pos1_ref[t]] = t
            src_ref[pos2_ref[t]] = t
            return carry
        lax.fori_loop(0, n_tokens, place, 0, unroll=4)

    def slab_copy(tile, slot, r):
        src = pl.multiple_of(src_ref[tile * tg + r] * nb, nb)
        dst = pl.multiple_of(r * nb, nb)
        return pltpu.make_async_copy(
            h_hbm.at[pl.ds(src, nb), :], buf_ref.at[slot, pl.ds(dst, nb), :], sems.at[slot])

    @pl.when(i < n_tiles)
    def _():
        def issue(r, carry):
            slab_copy(i, i % 2, r).start()
            return carry
        lax.fori_loop(0, tg, issue, 0, unroll=4)

    @pl.when(i > 0)
    def _():
        slot = (i - 1) % 2

        def drain(r, carry):
            slab_copy(i - 1, slot, r).wait()
            return carry
        lax.fori_loop(0, tg, drain, 0, unroll=4)
        half = nb * LANES
        for c in range(nb):
            lo, hi = _unpack_bf16_pairs(buf_ref[slot, pl.ds(c, tg, stride=nb), :])
            o_ref[:, c * LANES:(c + 1) * LANES] = lo
            o_ref[:, half + c * LANES:half + (c + 1) * LANES] = hi


def gather_rows(h_packed, pos1, pos2, n_rows, d, tg=256):
    nb = d // 2 // LANES
    n = h_packed.shape[0] // nb
    width = 2 * nb * LANES
    n_tiles = n_rows // tg
    grid_spec = pltpu.PrefetchScalarGridSpec(
        num_scalar_prefetch=2,
        grid=(n_tiles + 1,),
        in_specs=[pl.BlockSpec(memory_space=pl.ANY)],
        out_specs=pl.BlockSpec((tg, width), lambda i, p1, p2: (jnp.maximum(i - 1, 0), 0)),
        scratch_shapes=[pltpu.SMEM((n_rows,), I32), pltpu.VMEM((2, tg * nb, LANES), U32),
                        pltpu.SemaphoreType.DMA((2,))],
    )
    return pl.pallas_call(
        functools.partial(_gather_kernel, n_tokens=n, nb=nb),
        grid_spec=grid_spec,
        out_shape=jax.ShapeDtypeStruct((n_rows, width), BF16),
        compiler_params=_cparams(1),
        name="gather_rows",
    )(pos1, pos2, h_packed)


def _combine_kernel(pos1_ref, pos2_ref, o_hbm, gate_ref, x_ref, g_ref, out_ref, buf_ref, sems):
    tt = x_ref.shape[0]
    i = pl.program_id(0)
    n_tiles = pl.num_programs(0) - 1

    def row_copy(tile, slot, r, which, pos_ref):
        return pltpu.make_async_copy(
            o_hbm.at[pl.ds(pos_ref[tile * tt + r], 1), :],
            buf_ref.at[slot, which, pl.ds(r, 1), :], sems.at[slot])

    @pl.when(i < n_tiles)
    def _():
        def issue(r, carry):
            row_copy(i, i % 2, r, 0, pos1_ref).start()
            row_copy(i, i % 2, r, 1, pos2_ref).start()
            return carry
        lax.fori_loop(0, tt, issue, 0, unroll=4)

    @pl.when(i > 0)
    def _():
        slot = (i - 1) % 2

        def drain(r, carry):
            row_copy(i - 1, slot, r, 0, pos1_ref).wait()
            row_copy(i - 1, slot, r, 1, pos2_ref).wait()
            return carry
        lax.fori_loop(0, tt, drain, 0, unroll=4)
        gate = gate_ref[...]
        y = gate[:, 0:1] * buf_ref[slot, 0] + gate[:, 1:2] * buf_ref[slot, 1]
        out_ref[...] = x_ref[...] + (y * _rms(y)) * g_ref[...]


def combine(o, pos1, pos2, gate, x, g, tt=128):
    n, d = x.shape
    row = lambda i, p1, p2: (jnp.maximum(i - 1, 0), 0)
    grid_spec = pltpu.PrefetchScalarGridSpec(
        num_scalar_prefetch=2,
        grid=(n // tt + 1,),
        in_specs=[pl.BlockSpec(memory_space=pl.ANY),
                  pl.BlockSpec((tt, LANES), row),
                  pl.BlockSpec((tt, d), row),
                  pl.BlockSpec((1, d), lambda i, p1, p2: (0, 0))],
        out_specs=pl.BlockSpec((tt, d), row),
        scratch_shapes=[pltpu.VMEM((2, 2, tt, d), F32), pltpu.SemaphoreType.DMA((2,))],
    )
    return pl.pallas_call(
        _combine_kernel,
        grid_spec=grid_spec,
        out_shape=jax.ShapeDtypeStruct((n, d), F32),
        compiler_params=_cparams(1),
        name="combine",
    )(pos1, pos2, o, gate, x, g.reshape(1, d))


def _rope_tables(s):
    half = HEAD_DIM // 2
    inv = ROPE_BASE ** (-jnp.arange(half, dtype=F32) / half)
    ang = jnp.arange(s, dtype=F32)[:, None] * inv[None, :]
    cos, sin = jnp.cos(ang), jnp.sin(ang)
    return jnp.concatenate([cos, cos], axis=-1), jnp.concatenate([-sin, sin], axis=-1)


def _moe_schedule(ids, counts, tm, n_tiles, nc):
    ne = N_EXPERTS
    counts = counts[0, :ne]
    tiles = (counts + tm - 1) // tm
    nonempty = tiles > 0
    tile_end = jnp.cumsum(tiles)
    tile_start = tile_end - tiles
    pos1 = tile_start[ids[:, 0]] * tm + ids[:, 2]
    pos2 = tile_start[ids[:, 1]] * tm + ids[:, 3]

    steps = jnp.where(nonempty, jnp.maximum(tiles, nc), 0)
    step_end = jnp.cumsum(steps)
    step_start = step_end - steps
    total, n_used = step_end[-1], tile_end[-1]
    ar = jnp.arange(ne, dtype=I32)
    first_e = jnp.min(jnp.where(nonempty, ar, ne))
    last_e = jnp.max(jnp.where(nonempty, ar, -1))
    later = (ar[None, :] > ar[:, None]) & nonempty[None, :]
    nxt = jnp.min(jnp.where(later, ar[None, :], ne), axis=1)
    wrap = nxt == ne
    nxt = jnp.where(wrap, first_e, nxt)
    rank = jnp.cumsum(nonempty.astype(I32)) - 1

    s = jnp.arange(n_tiles + ne * (nc - 1), dtype=I32)
    in_blocks = s < total
    e = jnp.minimum(jnp.sum((s[:, None] >= step_end[None, :]).astype(I32), axis=1), ne - 1)
    e = jnp.where(in_blocks, e, last_e)
    p = s - step_start[e]
    is_comp = in_blocks & (p < tiles[e])
    z = s - total
    is_zero = (~in_blocks) & (z < n_tiles - n_used)
    group_last = tile_start[e] + tiles[e] - 1
    tile = jnp.where(is_comp, tile_start[e] + p,
                     jnp.where(is_zero, n_used + z, jnp.where(in_blocks, group_last, n_tiles - 1)))
    arow = jnp.where(is_comp, tile, jnp.where(in_blocks, group_last, n_used - 1))
    tail_rows = counts[e] - (tiles[e] - 1) * tm
    is_half = is_comp & (p == tiles[e] - 1) & (tail_rows <= tm // 2)
    kind = jnp.where(is_half, KIND_HALF,
                     jnp.where(is_comp, KIND_COMPUTE, jnp.where(is_zero, KIND_ZERO, KIND_NONE)))
    pf_chunk = jnp.where(in_blocks, jnp.minimum(p, nc - 1), nc - 1)
    pf_do = in_blocks & (p < nc)
    sched = (tile, arow, kind, rank[e], nxt[e], wrap[e], pf_chunk, pf_do,
             jnp.sum(nonempty.astype(I32)).reshape(1))
    return pos1.astype(I32), pos2.astype(I32), tuple(v.astype(I32) for v in sched)


def kernel(x, even_norm_mix_pre, even_w_in, even_b_forget, even_w_out, even_norm_mix_post,
           even_norm_ffn_pre, even_w_gate, even_w_up, even_w_down, even_norm_ffn_post,
           odd_norm_mix_pre, odd_w_pool, odd_pool_scale, odd_norm_mix_post, odd_norm_ffn_pre,
           odd_w_router, odd_we_gate, odd_we_up, odd_we_down, odd_norm_ffn_post):
    batch, seq, d = x.shape
    n = batch * seq
    x0 = x.reshape(n, d)
    ret_w = N_RET_HEADS * HEAD_DIM
    fox_w = N_FOX_HEADS * HEAD_DIM
    main_cols = 4 * ret_w + 3 * fox_w

    h0 = rms_cast(x0, even_norm_mix_pre[0])
    w_in_t = jnp.swapaxes(even_w_in, 1, 2)
    proj = gmm(h0, [w_in_t], tm=1024, tn=1024, n_out=main_cols, out_dtype=BF16, nc=4,
               transposed=True)
    b_forget = jnp.pad(even_b_forget[0], (0, LANES - N_FOX_HEADS)).reshape(1, LANES)
    f_logit = gmm(h0, [w_in_t], tm=512, tn=LANES, n_out=LANES, out_dtype=F32, nc=1,
                  transposed=True, w_col=lambda j: main_cols // LANES + j,
                  n_valid=N_FOX_HEADS)
    c_cols = forget_cumsum(f_logit, b_forget, batch)
    c_rows = c_cols[:, :N_FOX_HEADS].reshape(batch, seq, N_FOX_HEADS)
    c_rows = c_rows.transpose(0, 2, 1).reshape(batch, N_FOX_HEADS, 1, seq)
    cos, sin = _rope_tables(seq)
    log_gamma = jnp.log1p(-(2.0 ** (-5.0 - jnp.arange(N_RET_HEADS, dtype=F32))))
    ret = retention(proj, cos, sin, log_gamma, batch)
    fox = fox_attention(proj, c_cols, c_rows, batch)
    m = gmm([ret, fox], [even_w_out], tm=1024, tn=1024, n_out=d, out_dtype=F32, nc=8)
    x1, h1 = resid_norm(x0, m, even_norm_mix_post[0], even_norm_ffn_pre[0], "bf16")

    d_ff = even_w_gate.shape[-1]
    act = gmm(h1, [even_w_gate, even_w_up], tm=1024, tn=512, n_out=d_ff, out_dtype=BF16, nc=4)
    f = gmm(act, [even_w_down], tm=512, tn=512, n_out=d, out_dtype=F32, nc=16)
    x2, h2 = resid_norm(x1, f, even_norm_ffn_post[0], odd_norm_mix_pre[0], "f32")

    pooled = pool(h2, batch)
    group = d // len(POOL_WINDOWS)
    tn_pool = 512
    per = group // tn_pool
    m2 = gmm(pooled, [odd_w_pool[0]], tm=2048, tn=tn_pool, n_out=d, out_dtype=F32, nc=2,
             k=group, scale=odd_pool_scale[0].reshape(1, d),
             a_col=lambda j: j // per, w_group=lambda j: j // per, w_col=lambda j: j % per)
    w_router = jnp.pad(odd_w_router[0], ((0, 0), (0, LANES - N_EXPERTS)))
    x3, h3, logits = resid_norm(x2, m2, odd_norm_mix_post[0], odd_norm_ffn_pre[0], "packed",
                                w_router=w_router)

    ids, gate, counts = route(logits)
    tm = MOE_TILE
    n_rows = 2 * n + N_EXPERTS * tm
    pos1, pos2, sched = _moe_schedule(ids, counts, tm, n_rows // tm, MOE_CHUNKS)
    xs = gather_rows(h3, pos1, pos2, n_rows, d)
    d_fe = odd_we_gate.shape[-1]
    act2 = gmm(xs, [odd_we_gate[0], odd_we_up[0]], tm=tm, tn=512, n_out=d_fe,
               out_dtype=BF16, nc=MOE_CHUNKS, sched=sched)
    o = gmm(act2, [odd_we_down[0]], tm=tm, tn=512, n_out=d, out_dtype=F32,
            nc=MOE_CHUNKS, sched=sched)
    out = combine(o, pos1, pos2, gate, x3, odd_norm_ffn_post[0])
    return out.reshape(batch, seq, d)
```

```python
import functools

import jax
import jax.numpy as jnp
from jax import lax
from jax.experimental import pallas as pl
from jax.experimental.pallas import tpu as pltpu

F32 = jnp.float32
BF16 = jnp.bfloat16
U32 = jnp.uint32
I32 = jnp.int32

HEAD_DIM = 128
N_RET_HEADS = 16
N_FOX_HEADS = 16
RET_CHUNK = 128
ROPE_BASE = 10000.0
N_EXPERTS = 8
POOL_WINDOWS = (2, 4, 8, 16)
EPS = 1e-6
LANES = 128
NEG_BIG = -1e30

VMEM_LIMIT = 56 * 1024 * 1024
MOE_TILE = 256
MOE_CHUNKS = 8


def _cparams(n_axes):
    return pltpu.CompilerParams(
        dimension_semantics=("arbitrary",) * n_axes, vmem_limit_bytes=VMEM_LIMIT)


def _rms(v):
    return lax.rsqrt(jnp.mean(v * v, axis=-1, keepdims=True) + EPS)


def _sigmoid(v):
    return 1.0 / (1.0 + jnp.exp(-v))


def _rms_cast_kernel(x_ref, g_ref, o_ref):
    x = x_ref[...]
    o_ref[...] = ((x * _rms(x)) * g_ref[...]).astype(o_ref.dtype)


def rms_cast(x, g, tm=256):
    n, d = x.shape
    return pl.pallas_call(
        _rms_cast_kernel,
        grid=(n // tm,),
        in_specs=[pl.BlockSpec((tm, d), lambda i: (i, 0)),
                  pl.BlockSpec((1, d), lambda i: (0, 0))],
        out_specs=pl.BlockSpec((tm, d), lambda i: (i, 0)),
        out_shape=jax.ShapeDtypeStruct((n, d), BF16),
        compiler_params=_cparams(1),
        name="rms_cast",
    )(x, g.reshape(1, d))


def _pack_bf16_pairs(h):
    half = h.shape[1] // 2
    bits = lax.bitcast_convert_type(h.astype(BF16).astype(F32), U32)
    return (bits[:, :half] >> 16) | (bits[:, half:] & jnp.uint32(0xFFFF0000))


def _unpack_bf16_pairs(p):
    lo = lax.bitcast_convert_type(p << 16, F32).astype(BF16)
    hi = lax.bitcast_convert_type(p & jnp.uint32(0xFFFF0000), F32).astype(BF16)
    return lo, hi


def _resid_norm_kernel(*refs, h_mode, router):
    x_ref, m_ref, gp_ref = refs[:3]
    pos = 3
    gn_ref = wr_ref = None
    if h_mode is not None:
        gn_ref = refs[pos]; pos += 1
    if router:
        wr_ref = refs[pos]; pos += 1
    xo_ref = refs[pos]; pos += 1
    m = m_ref[...]
    xn = x_ref[...] + (m * _rms(m)) * gp_ref[...]
    xo_ref[...] = xn
    if h_mode is None:
        return
    h = (xn * _rms(xn)) * gn_ref[...]
    h_ref = refs[pos]; pos += 1
    if h_mode == "packed":
        packed = _pack_bf16_pairs(h)
        nb = packed.shape[1] // LANES
        for c in range(nb):
            h_ref[pl.ds(c, packed.shape[0], stride=nb), :] = packed[:, c * LANES:(c + 1) * LANES]
    else:
        h_ref[...] = h.astype(h_ref.dtype)
    if router:
        refs[pos][...] = jnp.dot(h, wr_ref[...], preferred_element_type=F32,
                                 precision=lax.Precision.HIGHEST)


def resid_norm(x, m, g_post, g_next=None, h_mode=None, w_router=None, tm=256):
    n, d = x.shape
    row = pl.BlockSpec((tm, d), lambda i: (i, 0))
    vec = pl.BlockSpec((1, d), lambda i: (0, 0))
    args = [x, m, g_post.reshape(1, d)]
    in_specs = [row, row, vec]
    out_shape = [jax.ShapeDtypeStruct((n, d), F32)]
    out_specs = [row]
    if h_mode is not None:
        args.append(g_next.reshape(1, d)); in_specs.append(vec)
        if h_mode == "packed":
            nb = d // 2 // LANES
            out_shape.append(jax.ShapeDtypeStruct((n * nb, LANES), U32))
            out_specs.append(pl.BlockSpec((tm * nb, LANES), lambda i: (i, 0)))
        else:
            out_shape.append(jax.ShapeDtypeStruct((n, d), BF16 if h_mode == "bf16" else F32))
            out_specs.append(row)
    router = w_router is not None
    if router:
        args.append(w_router)
        in_specs.append(pl.BlockSpec((d, LANES), lambda i: (0, 0)))
        out_shape.append(jax.ShapeDtypeStruct((n, LANES), F32))
        out_specs.append(pl.BlockSpec((tm, LANES), lambda i: (i, 0)))
    return pl.pallas_call(
        functools.partial(_resid_norm_kernel, h_mode=h_mode, router=router),
        grid=(n // tm,),
        in_specs=in_specs, out_specs=out_specs, out_shape=out_shape,
        compiler_params=_cparams(1),
        name="resid_norm",
    )(*args)


N_SCHED = 9
KIND_NONE, KIND_COMPUTE, KIND_ZERO = 0, 1, 2


def _gmm_kernel(tile_ref, arow_ref, kind_ref, blk_ref, pfg_ref, pfw_ref, pfc_ref, pfd_ref,
                nblk_ref, *refs, n_a, n_w, n_col, has_scale, transposed, n_valid, merge_cast):
    del tile_ref, arow_ref, pfg_ref
    a_refs = refs[:n_a]
    c_refs = refs[n_a:n_a + n_w]
    pos = n_a + n_w
    s_ref = None
    if has_scale:
        s_ref = refs[pos]; pos += 1
    o_ref = refs[pos]; pos += 1
    slots = (refs[pos:pos + n_w], refs[pos + n_w:pos + 2 * n_w])
    jj = pl.program_id(0)
    s = pl.program_id(1)
    parity = (jj * nblk_ref[0] + blk_ref[s] + nblk_ref[0]) % 2
    rows_per_chunk = c_refs[0].shape[1]
    tcol = jj - 1 + pfw_ref[s]
    chunk_due = (pfd_ref[s] == 1) & (tcol >= 0) & (tcol < n_col)
    computing = (kind_ref[s] == KIND_COMPUTE) & (jj >= 1)

    def cast_chunk(nxt):
        r = pl.multiple_of(pfc_ref[s] * rows_per_chunk, rows_per_chunk)
        for c_ref, wbf in zip(c_refs, nxt):
            wbf[pl.ds(r, rows_per_chunk), :] = c_ref[0].astype(BF16)

    def matmul(wbf):
        if transposed:
            return lax.dot_general(a_refs[0][...], wbf[...], (((1,), (1,)), ((), ())),
                                   preferred_element_type=F32)
        acc = None
        off = 0
        for a_ref in a_refs:
            ka = a_ref.shape[1]
            part = jnp.dot(a_ref[...], wbf[off:off + ka, :], preferred_element_type=F32)
            acc = part if acc is None else acc + part
            off += ka
        return acc

    def compute(cur, nxt):
        if merge_cast:
            cast_chunk(nxt)
        acc = matmul(cur[0])
        if n_w == 2:
            acc = (acc * _sigmoid(acc)) * matmul(cur[1])
        if has_scale:
            acc = acc * s_ref[...]
        if n_valid is not None:
            col = lax.broadcasted_iota(I32, acc.shape, 1) + (jj - 1) * acc.shape[1]
            acc = jnp.where(col < n_valid, acc, 0.0)
        o_ref[...] = acc.astype(o_ref.dtype)

    for p in (0, 1):
        cur, nxt = slots[p], slots[1 - p]
        pl.when(computing & (parity == p))(functools.partial(compute, cur, nxt))
        cast_here = chunk_due & jnp.logical_not(computing) if merge_cast else chunk_due
        pl.when(cast_here & (parity == p))(functools.partial(cast_chunk, nxt))

    @pl.when((kind_ref[s] == KIND_ZERO) & (jj >= 1))
    def _():
        o_ref[...] = jnp.zeros_like(o_ref)


def _dense_schedule(n_tiles, nc):
    t = jnp.arange(n_tiles, dtype=I32)
    zeros = jnp.zeros((n_tiles,), I32)
    ones = jnp.ones((n_tiles,), I32)
    return (t, t, ones, zeros, zeros, ones, jnp.minimum(t, nc - 1), (t < nc).astype(I32),
            jnp.ones((1,), I32))


def gmm(a, ws, *, tm, tn, n_out, out_dtype, nc, sched=None, k=None, scale=None,
        a_col=None, w_group=None, w_col=None, transposed=False, n_valid=None,
        merge_cast=True):
    a_parts = list(a) if isinstance(a, (list, tuple)) else [a]
    rows = a_parts[0].shape[0]
    if k is None:
        k = sum(p.shape[1] for p in a_parts)
    if sched is None:
        sched = _dense_schedule(rows // tm, nc)
    n_steps = sched[0].shape[0]
    n_col = pl.cdiv(n_out, tn)
    a_col = a_col or (lambda j: 0)
    w_col = w_col or (lambda j: j)

    def cur_col(jj):
        return jnp.maximum(jj - 1, 0)

    def a_map(jj, s, tile, arow, *_):
        return (jnp.where(jj == 0, arow[0], arow[s]), a_col(cur_col(jj)))

    def o_map(jj, s, tile, *_):
        return (jnp.where(jj == 0, tile[0], tile[s]), cur_col(jj))

    def c_map(jj, s, tile, arow, kind, blk, pfg, pfw, pfc, *_):
        col = jnp.clip(jj - 1 + pfw[s], 0, n_col - 1)
        group = pfg[s] if w_group is None else w_group(col)
        if transposed:
            return (group, w_col(col) * nc + pfc[s], 0)
        return (group, pfc[s], w_col(col))

    in_specs = [pl.BlockSpec((tm, k if len(a_parts) == 1 else p.shape[1]), a_map)
                for p in a_parts]
    if transposed:
        assert len(a_parts) == 1 and tn % nc == 0
        chunk, wbf_shape = (1, tn // nc, k), (tn, k)
    else:
        assert k % nc == 0
        chunk, wbf_shape = (1, k // nc, tn), (k, tn)
    in_specs += [pl.BlockSpec(chunk, c_map) for _ in ws]
    args = [*a_parts, *ws]
    if scale is not None:
        in_specs.append(pl.BlockSpec((1, tn), lambda jj, s, *_: (0, cur_col(jj))))
        args.append(scale)
    grid_spec = pltpu.PrefetchScalarGridSpec(
        num_scalar_prefetch=N_SCHED,
        grid=(n_col + 1, n_steps),
        in_specs=in_specs,
        out_specs=pl.BlockSpec((tm, tn), o_map),
        scratch_shapes=[pltpu.VMEM(wbf_shape, BF16) for _ in range(2 * len(ws))],
    )
    return pl.pallas_call(
        functools.partial(_gmm_kernel, n_a=len(a_parts), n_w=len(ws), n_col=n_col,
                          has_scale=scale is not None, transposed=transposed,
                          n_valid=n_valid, merge_cast=merge_cast),
        grid_spec=grid_spec,
        out_shape=jax.ShapeDtypeStruct((rows, n_out), out_dtype),
        compiler_params=_cparams(2),
        name="gmm",
    )(*sched, *args)


def _forget_cumsum_kernel(f_ref, b_ref, o_ref):
    x = f_ref[...] + b_ref[...]
    ls = jnp.minimum(x, 0.0) - jnp.log1p(jnp.exp(-jnp.abs(x)))
    s = ls.shape[0]
    row = lax.broadcasted_iota(I32, ls.shape, 0)
    sh = 1
    while sh < s:
        ls = ls + jnp.where(row >= sh, pltpu.roll(ls, sh, 0), 0.0)
        sh *= 2
    o_ref[...] = ls


def forget_cumsum(f, b, batch):
    n, w = f.shape
    s = n // batch
    return pl.pallas_call(
        _forget_cumsum_kernel,
        grid=(batch,),
        in_specs=[pl.BlockSpec((s, w), lambda i: (i, 0)),
                  pl.BlockSpec((1, w), lambda i: (0, 0))],
        out_specs=pl.BlockSpec((s, w), lambda i: (i, 0)),
        out_shape=jax.ShapeDtypeStruct((n, w), F32),
        compiler_params=_cparams(1),
        name="forget_cumsum",
    )(f, b)


def _retention_kernel(lg_ref, q_ref, k_ref, v_ref, g_ref, cos_ref, sin_ref, o_ref):
    c = RET_CHUNK
    d = HEAD_DIM
    s = q_ref.shape[0]
    lg = lg_ref[pl.program_id(1)]
    ii = lax.broadcasted_iota(I32, (c, c), 0)
    jj = lax.broadcasted_iota(I32, (c, c), 1)
    diff = (ii - jj).astype(F32)
    decay_in = jnp.where(diff >= 0, jnp.exp(lg * jnp.maximum(diff, 0.0)), 0.0)
    jc = lax.broadcasted_iota(I32, (c, 1), 0).astype(F32)
    zeta = jnp.exp(lg * (c - 1 - jc))
    q_decay = jnp.exp(lg * (jc + 1.0))
    gamma_c = jnp.exp(jnp.full((1, 1), lg, F32) * c)
    scale = d ** -0.5
    nt = (((1,), (1,)), ((), ()))
    tn = (((0,), (0,)), ((), ()))
    state = jnp.zeros((d, d), F32)
    for n in range(s // c):
        sl = pl.ds(n * c, c)
        cos = cos_ref[sl, :]
        sin = sin_ref[sl, :]
        q = q_ref[sl, :].astype(F32)
        k = k_ref[sl, :].astype(F32)
        v = v_ref[sl, :]
        qr = q * cos + pltpu.roll(q, d // 2, 1) * sin
        kr = (k * cos + pltpu.roll(k, d // 2, 1) * sin) * scale
        scores = lax.dot_general(qr.astype(BF16), kr.astype(BF16), nt,
                                 preferred_element_type=F32) * decay_in
        inner = jnp.dot(scores.astype(BF16), v, preferred_element_type=F32)
        cross = jnp.dot((qr * q_decay).astype(BF16), state.astype(BF16),
                        preferred_element_type=F32)
        kv = lax.dot_general((kr * zeta).astype(BF16), v, tn, preferred_element_type=F32)
        state = gamma_c * state + kv
        o = inner + cross
        xc = o - jnp.mean(o, axis=-1, keepdims=True)
        y = xc * lax.rsqrt(jnp.mean(xc * xc, axis=-1, keepdims=True) + EPS)
        g = g_ref[sl, :].astype(F32)
        o_ref[sl, :] = ((g * _sigmoid(g)) * y).astype(o_ref.dtype)


def retention(proj, cos, sin, log_gamma, batch):
    n = proj.shape[0]
    s = n // batch
    d = HEAD_DIM
    hh = N_RET_HEADS

    def col(off):
        return pl.BlockSpec((s, d), lambda b, h, lg: (b, off + h))

    tab = pl.BlockSpec((s, d), lambda b, h, lg: (0, 0))
    grid_spec = pltpu.PrefetchScalarGridSpec(
        num_scalar_prefetch=1,
        grid=(batch, hh),
        in_specs=[col(0), col(hh), col(2 * hh), col(3 * hh), tab, tab],
        out_specs=pl.BlockSpec((s, d), lambda b, h, lg: (b, h)),
    )
    return pl.pallas_call(
        _retention_kernel,
        grid_spec=grid_spec,
        out_shape=jax.ShapeDtypeStruct((n, hh * d), BF16),
        compiler_params=_cparams(2),
        name="retention",
    )(log_gamma, proj, proj, proj, proj, cos, sin)


def _fox_kernel(q_ref, k_ref, v_ref, ccol_ref, crow_ref, o_ref, *, tq):
    s, d = q_ref.shape
    h = pl.program_id(1)
    scale = d ** -0.5
    nt = (((1,), (1,)), ((), ()))
    lane = lax.broadcasted_iota(I32, (tq, LANES), 1)
    rr = lax.broadcasted_iota(I32, (tq, tq), 0)
    cc = lax.broadcasted_iota(I32, (tq, tq), 1)

    for qi in range(s // tq):
        qs = pl.ds(qi * tq, tq)
        q = q_ref[qs, :]
        cq = jnp.sum(jnp.where(lane == h, ccol_ref[qs, :], 0.0), axis=-1, keepdims=True)

        def scores(kb):
            ks = pl.ds(pl.multiple_of(kb * tq, tq), tq)
            sc = lax.dot_general(q, k_ref[ks, :], nt, preferred_element_type=F32) * scale
            return sc + cq - crow_ref[:, ks], v_ref[ks, :]

        def update(carry, sc, v):
            m, l, acc = carry
            m_new = jnp.maximum(m, jnp.max(sc, axis=-1, keepdims=True))
            alpha = jnp.exp(m - m_new)
            p = jnp.exp(sc - m_new)
            l = alpha * l + jnp.sum(p, axis=-1, keepdims=True)
            acc = alpha * acc + jnp.dot(p.astype(BF16), v, preferred_element_type=F32)
            return m_new, l, acc

        def body(kb, carry):
            sc, v = scores(kb)
            return update(carry, sc, v)

        carry = (jnp.full((tq, 1), NEG_BIG, F32), jnp.zeros((tq, 1), F32),
                 jnp.zeros((tq, d), F32))
        carry = lax.fori_loop(0, qi, body, carry, unroll=True)
        sc, v = scores(qi)
        _, l, acc = update(carry, jnp.where(rr >= cc, sc, NEG_BIG), v)
        o_ref[qs, :] = (acc / l).astype(o_ref.dtype)


def fox_attention(proj, c_cols, c_rows, batch, tq=256):
    n = proj.shape[0]
    s = n // batch
    d = HEAD_DIM
    base = 4 * N_RET_HEADS
    hh = N_FOX_HEADS

    def col(off):
        return pl.BlockSpec((s, d), lambda b, h: (b, off + h))

    return pl.pallas_call(
        functools.partial(_fox_kernel, tq=tq),
        grid=(batch, hh),
        in_specs=[col(base), col(base + hh), col(base + 2 * hh),
                  pl.BlockSpec((s, LANES), lambda b, h: (b, 0)),
                  pl.BlockSpec((None, None, 1, s), lambda b, h: (b, h, 0, 0))],
        out_specs=pl.BlockSpec((s, d), lambda b, h: (b, h)),
        out_shape=jax.ShapeDtypeStruct((n, hh * d), BF16),
        compiler_params=_cparams(2),
        name="fox_attention",
    )(proj, proj, proj, c_cols, c_rows)


def _pool_kernel(h_ref, o_ref, *, blocks_per_group):
    x = h_ref[...]
    row = lax.broadcasted_iota(I32, x.shape, 0)
    group = pl.program_id(1) // blocks_per_group

    def shifted(a, by):
        return jnp.where(row >= by, pltpu.roll(a, by, 0), 0.0)

    for gi, w in enumerate(POOL_WINDOWS):
        @pl.when(group == gi)
        def _():
            acc = x
            by = 1
            while by < w:
                acc = acc + shifted(acc, by)
                by *= 2
            count = jnp.minimum(row + 1, w).astype(F32)
            o_ref[...] = (acc / count - x).astype(o_ref.dtype)


def pool(h, batch, tc=256):
    n, d = h.shape
    s = n // batch
    group = d // len(POOL_WINDOWS)
    return pl.pallas_call(
        functools.partial(_pool_kernel, blocks_per_group=group // tc),
        grid=(batch, d // tc),
        in_specs=[pl.BlockSpec((s, tc), lambda b, j: (b, j))],
        out_specs=pl.BlockSpec((s, tc), lambda b, j: (b, j)),
        out_shape=jax.ShapeDtypeStruct((n, d), BF16),
        compiler_params=_cparams(2),
        name="pool",
    )(h)


def _route_kernel(lg_ref, ids_ref, gate_ref, cnt_ref, carry_ref):
    tt = lg_ref.shape[0]

    @pl.when(pl.program_id(0) == 0)
    def _():
        carry_ref[...] = jnp.zeros_like(carry_ref)

    lane = lax.broadcasted_iota(I32, (tt, LANES), 1)
    lane_f = lane.astype(F32)
    x = jnp.where(lane < N_EXPERTS, lg_ref[...], -jnp.inf)
    v1 = jnp.max(x, axis=-1, keepdims=True)
    i1 = jnp.min(jnp.where(x == v1, lane_f, float(LANES)), axis=-1, keepdims=True)
    x2 = jnp.where(lane_f == i1, -jnp.inf, x)
    v2 = jnp.max(x2, axis=-1, keepdims=True)
    i2 = jnp.min(jnp.where(x2 == v2, lane_f, float(LANES)), axis=-1, keepdims=True)
    e = jnp.exp(v2 - v1)
    w1 = 1.0 / (1.0 + e)
    w2 = e / (1.0 + e)
    sel1 = lane_f == i1
    sel2 = lane_f == i2
    onehot = jnp.where(sel1, 1.0, jnp.where(sel2, 1.0, 0.0))
    rr = lax.broadcasted_iota(I32, (tt, tt), 0)
    cc = lax.broadcasted_iota(I32, (tt, tt), 1)
    tri = jnp.where(rr > cc, 1.0, 0.0).astype(BF16)
    before = jnp.dot(tri, onehot.astype(BF16), preferred_element_type=F32) + carry_ref[0:1, :]
    r1 = jnp.sum(jnp.where(sel1, before, 0.0), axis=-1, keepdims=True)
    r2 = jnp.sum(jnp.where(sel2, before, 0.0), axis=-1, keepdims=True)
    ids = jnp.where(lane == 0, i1, jnp.where(lane == 1, i2,
                    jnp.where(lane == 2, r1, jnp.where(lane == 3, r2, 0.0))))
    ids_ref[...] = ids.astype(I32)
    gate_ref[...] = jnp.where(lane == 0, w1, jnp.where(lane == 1, w2, 0.0))
    total = carry_ref[0:1, :] + jnp.sum(onehot, axis=0, keepdims=True)
    carry_ref[...] = jnp.broadcast_to(total, carry_ref.shape)
    cnt_ref[...] = jnp.broadcast_to(total, cnt_ref.shape).astype(I32)


def route(logits, tt=256):
    n = logits.shape[0]
    row = pl.BlockSpec((tt, LANES), lambda i: (i, 0))
    return pl.pallas_call(
        _route_kernel,
        grid=(n // tt,),
        in_specs=[row],
        out_specs=[row, row, pl.BlockSpec((8, LANES), lambda i: (0, 0))],
        out_shape=[jax.ShapeDtypeStruct((n, LANES), I32),
                   jax.ShapeDtypeStruct((n, LANES), F32),
                   jax.ShapeDtypeStruct((8, LANES), I32)],
        scratch_shapes=[pltpu.VMEM((8, LANES), F32)],
        compiler_params=_cparams(1),
        name="route",
    )(logits)


def _gather_kernel(pos1_ref, pos2_ref, h_hbm, o_ref, src_ref, buf_ref, sems, *,
                   n_tokens, nb):
    tg = o_ref.shape[0]
    i = pl.program_id(0)
    n_tiles = pl.num_programs(0) - 1

    @pl.when(i == 0)
    def _():
        def clear(p, carry):
            src_ref[p] = 0
            return carry
        lax.fori_loop(0, src_ref.shape[0], clear, 0, unroll=8)

        def place(t, carry):
            src_ref[pos1_ref[t]] = t
            src_ref[pos2_ref[t]] = t
            return carry
        lax.fori_loop(0, n_tokens, place, 0, unroll=4)

    def slab_copy(tile, slot, r):
        src = pl.multiple_of(src_ref[tile * tg + r] * nb, nb)
        dst = pl.multiple_of(r * nb, nb)
        return pltpu.make_async_copy(
            h_hbm.at[pl.ds(src, nb), :], buf_ref.at[slot, pl.ds(dst, nb), :], sems.at[slot])

    @pl.when(i < n_tiles)
    def _():
        def issue(r, carry):
            slab_copy(i, i % 2, r).start()
            return carry
        lax.fori_loop(0, tg, issue, 0, unroll=4)

    @pl.when(i > 0)
    def _():
        slot = (i - 1) % 2

        def drain(r, carry):
            slab_copy(i - 1, slot, r).wait()
            return carry
        lax.fori_loop(0, tg, drain, 0, unroll=4)
        half = nb * LANES
        for c in range(nb):
            lo, hi = _unpack_bf16_pairs(buf_ref[slot, pl.ds(c, tg, stride=nb), :])
            o_ref[:, c * LANES:(c + 1) * LANES] = lo
            o_ref[:, half + c * LANES:half + (c + 1) * LANES] = hi


def gather_rows(h_packed, pos1, pos2, n_rows, d, tg=256):
    nb = d // 2 // LANES
    n = h_packed.shape[0] // nb
    width = 2 * nb * LANES
    n_tiles = n_rows // tg
    grid_spec = pltpu.PrefetchScalarGridSpec(
        num_scalar_prefetch=2,
        grid=(n_tiles + 1,),
        in_specs=[pl.BlockSpec(memory_space=pl.ANY)],
        out_specs=pl.BlockSpec((tg, width), lambda i, p1, p2: (jnp.maximum(i - 1, 0), 0)),
        scratch_shapes=[pltpu.SMEM((n_rows,), I32), pltpu.VMEM((2, tg * nb, LANES), U32),
                        pltpu.SemaphoreType.DMA((2,))],
    )
    return pl.pallas_call(
        functools.partial(_gather_kernel, n_tokens=n, nb=nb),
        grid_spec=grid_spec,
        out_shape=jax.ShapeDtypeStruct((n_rows, width), BF16),
        compiler_params=_cparams(1),
        name="gather_rows",
    )(pos1, pos2, h_packed)


def _combine_kernel(pos1_ref, pos2_ref, o_hbm, gate_ref, x_ref, g_ref, out_ref, buf_ref, sems):
    tt = x_ref.shape[0]
    i = pl.program_id(0)
    n_tiles = pl.num_programs(0) - 1

    def row_copy(tile, slot, r, which, pos_ref):
        return pltpu.make_async_copy(
            o_hbm.at[pl.ds(pos_ref[tile * tt + r], 1), :],
            buf_ref.at[slot, which, pl.ds(r, 1), :], sems.at[slot])

    @pl.when(i < n_tiles)
    def _():
        def issue(r, carry):
            row_copy(i, i % 2, r, 0, pos1_ref).start()
            row_copy(i, i % 2, r, 1, pos2_ref).start()
            return carry
        lax.fori_loop(0, tt, issue, 0, unroll=4)

    @pl.when(i > 0)
    def _():
        slot = (i - 1) % 2

        def drain(r, carry):
            row_copy(i - 1, slot, r, 0, pos1_ref).wait()
            row_copy(i - 1, slot, r, 1, pos2_ref).wait()
            return carry
        lax.fori_loop(0, tt, drain, 0, unroll=4)
        gate = gate_ref[...]
        y = gate[:, 0:1] * buf_ref[slot, 0] + gate[:, 1:2] * buf_ref[slot, 1]
        out_ref[...] = x_ref[...] + (y * _rms(y)) * g_ref[...]


def combine(o, pos1, pos2, gate, x, g, tt=128):
    n, d = x.shape
    row = lambda i, p1, p2: (jnp.maximum(i - 1, 0), 0)
    grid_spec = pltpu.PrefetchScalarGridSpec(
        num_scalar_prefetch=2,
        grid=(n // tt + 1,),
        in_specs=[pl.BlockSpec(memory_space=pl.ANY),
                  pl.BlockSpec((tt, LANES), row),
                  pl.BlockSpec((tt, d), row),
                  pl.BlockSpec((1, d), lambda i, p1, p2: (0, 0))],
        out_specs=pl.BlockSpec((tt, d), row),
        scratch_shapes=[pltpu.VMEM((2, 2, tt, d), F32), pltpu.SemaphoreType.DMA((2,))],
    )
    return pl.pallas_call(
        _combine_kernel,
        grid_spec=grid_spec,
        out_shape=jax.ShapeDtypeStruct((n, d), F32),
        compiler_params=_cparams(1),
        name="combine",
    )(pos1, pos2, o, gate, x, g.reshape(1, d))


def _rope_tables(s):
    half = HEAD_DIM // 2
    inv = ROPE_BASE ** (-jnp.arange(half, dtype=F32) / half)
    ang = jnp.arange(s, dtype=F32)[:, None] * inv[None, :]
    cos, sin = jnp.cos(ang), jnp.sin(ang)
    return jnp.concatenate([cos, cos], axis=-1), jnp.concatenate([-sin, sin], axis=-1)


def _moe_schedule(ids, counts, tm, n_tiles, nc):
    ne = N_EXPERTS
    counts = counts[0, :ne]
    tiles = (counts + tm - 1) // tm
    nonempty = tiles > 0
    tile_end = jnp.cumsum(tiles)
    tile_start = tile_end - tiles
    pos1 = tile_start[ids[:, 0]] * tm + ids[:, 2]
    pos2 = tile_start[ids[:, 1]] * tm + ids[:, 3]

    steps = jnp.where(nonempty, jnp.maximum(tiles, nc), 0)
    step_end = jnp.cumsum(steps)
    step_start = step_end - steps
    total, n_used = step_end[-1], tile_end[-1]
    ar = jnp.arange(ne, dtype=I32)
    first_e = jnp.min(jnp.where(nonempty, ar, ne))
    last_e = jnp.max(jnp.where(nonempty, ar, -1))
    later = (ar[None, :] > ar[:, None]) & nonempty[None, :]
    nxt = jnp.min(jnp.where(later, ar[None, :], ne), axis=1)
    wrap = nxt == ne
    nxt = jnp.where(wrap, first_e, nxt)
    rank = jnp.cumsum(nonempty.astype(I32)) - 1

    s = jnp.arange(n_tiles + ne * (nc - 1), dtype=I32)
    in_blocks = s < total
    e = jnp.minimum(jnp.sum((s[:, None] >= step_end[None, :]).astype(I32), axis=1), ne - 1)
    e = jnp.where(in_blocks, e, last_e)
    p = s - step_start[e]
    is_comp = in_blocks & (p < tiles[e])
    z = s - total
    is_zero = (~in_blocks) & (z < n_tiles - n_used)
    group_last = tile_start[e] + tiles[e] - 1
    tile = jnp.where(is_comp, tile_start[e] + p,
                     jnp.where(is_zero, n_used + z, jnp.where(in_blocks, group_last, n_tiles - 1)))
    arow = jnp.where(is_comp, tile, jnp.where(in_blocks, group_last, n_used - 1))
    kind = jnp.where(is_comp, KIND_COMPUTE, jnp.where(is_zero, KIND_ZERO, KIND_NONE))
    pf_chunk = jnp.where(in_blocks, jnp.minimum(p, nc - 1), nc - 1)
    pf_do = in_blocks & (p < nc)
    sched = (tile, arow, kind, rank[e], nxt[e], wrap[e], pf_chunk, pf_do,
             jnp.sum(nonempty.astype(I32)).reshape(1))
    return pos1.astype(I32), pos2.astype(I32), tuple(v.astype(I32) for v in sched)


def kernel(x, even_norm_mix_pre, even_w_in, even_b_forget, even_w_out, even_norm_mix_post,
           even_norm_ffn_pre, even_w_gate, even_w_up, even_w_down, even_norm_ffn_post,
           odd_norm_mix_pre, odd_w_pool, odd_pool_scale, odd_norm_mix_post, odd_norm_ffn_pre,
           odd_w_router, odd_we_gate, odd_we_up, odd_we_down, odd_norm_ffn_post):
    batch, seq, d = x.shape
    n = batch * seq
    x0 = x.reshape(n, d)
    ret_w = N_RET_HEADS * HEAD_DIM
    fox_w = N_FOX_HEADS * HEAD_DIM
    main_cols = 4 * ret_w + 3 * fox_w

    h0 = rms_cast(x0, even_norm_mix_pre[0])
    w_in_t = jnp.swapaxes(even_w_in, 1, 2)
    proj = gmm(h0, [w_in_t], tm=1024, tn=1024, n_out=main_cols, out_dtype=BF16, nc=4,
               transposed=True)
    b_forget = jnp.pad(even_b_forget[0], (0, LANES - N_FOX_HEADS)).reshape(1, LANES)
    f_logit = gmm(h0, [w_in_t], tm=512, tn=LANES, n_out=LANES, out_dtype=F32, nc=1,
                  transposed=True, w_col=lambda j: main_cols // LANES + j,
                  n_valid=N_FOX_HEADS)
    c_cols = forget_cumsum(f_logit, b_forget, batch)
    c_rows = c_cols[:, :N_FOX_HEADS].reshape(batch, seq, N_FOX_HEADS)
    c_rows = c_rows.transpose(0, 2, 1).reshape(batch, N_FOX_HEADS, 1, seq)
    cos, sin = _rope_tables(seq)
    log_gamma = jnp.log1p(-(2.0 ** (-5.0 - jnp.arange(N_RET_HEADS, dtype=F32))))
    ret = retention(proj, cos, sin, log_gamma, batch)
    fox = fox_attention(proj, c_cols, c_rows, batch)
    m = gmm([ret, fox], [even_w_out], tm=1024, tn=1024, n_out=d, out_dtype=F32, nc=8)
    x1, h1 = resid_norm(x0, m, even_norm_mix_post[0], even_norm_ffn_pre[0], "bf16")

    d_ff = even_w_gate.shape[-1]
    act = gmm(h1, [even_w_gate, even_w_up], tm=1024, tn=512, n_out=d_ff, out_dtype=BF16, nc=4)
    f = gmm(act, [even_w_down], tm=512, tn=512, n_out=d, out_dtype=F32, nc=16)
    x2, h2 = resid_norm(x1, f, even_norm_ffn_post[0], odd_norm_mix_pre[0], "f32")

    pooled = pool(h2, batch)
    group = d // len(POOL_WINDOWS)
    tn_pool = 512
    per = group // tn_pool
    m2 = gmm(pooled, [odd_w_pool[0]], tm=2048, tn=tn_pool, n_out=d, out_dtype=F32, nc=2,
             k=group, scale=odd_pool_scale[0].reshape(1, d),
             a_col=lambda j: j // per, w_group=lambda j: j // per, w_col=lambda j: j % per)
    w_router = jnp.pad(odd_w_router[0], ((0, 0), (0, LANES - N_EXPERTS)))
    x3, h3, logits = resid_norm(x2, m2, odd_norm_mix_post[0], odd_norm_ffn_pre[0], "packed",
                                w_router=w_router)

    ids, gate, counts = route(logits)
    tm = MOE_TILE
    n_rows = 2 * n + N_EXPERTS * tm
    pos1, pos2, sched = _moe_schedule(ids, counts, tm, n_rows // tm, MOE_CHUNKS)
    xs = gather_rows(h3, pos1, pos2, n_rows, d)
    d_fe = odd_we_gate.shape[-1]
    act2 = gmm(xs, [odd_we_gate[0], odd_we_up[0]], tm=tm, tn=1024, n_out=d_fe,
               out_dtype=BF16, nc=MOE_CHUNKS, sched=sched, merge_cast=False)
    o = gmm(act2, [odd_we_down[0]], tm=tm, tn=1024, n_out=d, out_dtype=F32,
            nc=MOE_CHUNKS, sched=sched, merge_cast=False)
    out = combine(o, pos1, pos2, gate, x3, odd_norm_ffn_post[0])
    return out.reshape(batch, seq, d)
```

```python
import functools

import jax
import jax.numpy as jnp
from jax import lax
from jax.experimental import pallas as pl
from jax.experimental.pallas import tpu as pltpu

F32 = jnp.float32
BF16 = jnp.bfloat16
U32 = jnp.uint32
I32 = jnp.int32

HEAD_DIM = 128
N_RET_HEADS = 16
N_FOX_HEADS = 16
RET_CHUNK = 128
ROPE_BASE = 10000.0
N_EXPERTS = 8
POOL_WINDOWS = (2, 4, 8, 16)
EPS = 1e-6
LANES = 128
NEG_BIG = -1e30

VMEM_LIMIT = 56 * 1024 * 1024
MOE_TILE = 256
MOE_CHUNKS = 8


def _cparams(n_axes):
    return pltpu.CompilerParams(
        dimension_semantics=("arbitrary",) * n_axes, vmem_limit_bytes=VMEM_LIMIT)


def _rms(v):
    return lax.rsqrt(jnp.mean(v * v, axis=-1, keepdims=True) + EPS)


def _sigmoid(v):
    return 1.0 / (1.0 + jnp.exp(-v))


def _rms_cast_kernel(x_ref, g_ref, o_ref):
    x = x_ref[...]
    o_ref[...] = ((x * _rms(x)) * g_ref[...]).astype(o_ref.dtype)


def rms_cast(x, g, tm=256):
    n, d = x.shape
    return pl.pallas_call(
        _rms_cast_kernel,
        grid=(n // tm,),
        in_specs=[pl.BlockSpec((tm, d), lambda i: (i, 0)),
                  pl.BlockSpec((1, d), lambda i: (0, 0))],
        out_specs=pl.BlockSpec((tm, d), lambda i: (i, 0)),
        out_shape=jax.ShapeDtypeStruct((n, d), BF16),
        compiler_params=_cparams(1),
        name="rms_cast",
    )(x, g.reshape(1, d))


def _pack_bf16_pairs(h):
    half = h.shape[1] // 2
    bits = lax.bitcast_convert_type(h.astype(BF16).astype(F32), U32)
    return (bits[:, :half] >> 16) | (bits[:, half:] & jnp.uint32(0xFFFF0000))


def _unpack_bf16_pairs(p):
    lo = lax.bitcast_convert_type(p << 16, F32).astype(BF16)
    hi = lax.bitcast_convert_type(p & jnp.uint32(0xFFFF0000), F32).astype(BF16)
    return lo, hi


def _pool_tile(h, o_ref, halo_ref, tiles_per_seq):
    tm, d = h.shape
    group = d // len(POOL_WINDOWS)
    seq_tile = pl.program_id(0) % tiles_per_seq

    @pl.when(seq_tile == 0)
    def _():
        halo_ref[...] = jnp.zeros_like(halo_ref)

    row = lax.broadcasted_iota(I32, (tm, group), 0)
    row8 = lax.broadcasted_iota(I32, (8, group), 0)
    count_cap = seq_tile * tm + row + 1
    for gi, w in enumerate(POOL_WINDOWS):
        cols = slice(gi * group, (gi + 1) * group)
        x = h[:, cols]
        acc = x
        by, level = 1, 0
        while by < w:
            prev = halo_ref[level, :, cols]
            rolled = pltpu.roll(acc, by, 0)
            prev_rolled = pltpu.roll(prev, by, 0) if by < 8 else prev
            top = jnp.where(row8 < by, prev_rolled, rolled[0:8])
            halo_ref[level, :, cols] = acc[tm - 8:tm]
            acc = acc + jnp.concatenate([top, rolled[8:]], axis=0)
            by *= 2
            level += 1
        count = jnp.minimum(count_cap, w).astype(F32)
        o_ref[:, cols] = (acc / count - x).astype(o_ref.dtype)


def _resid_norm_kernel(*refs, h_mode, router, tiles_per_seq):
    x_ref, m_ref, gp_ref = refs[:3]
    pos = 3
    gn_ref = wr_ref = None
    if h_mode is not None:
        gn_ref = refs[pos]; pos += 1
    if router:
        wr_ref = refs[pos]; pos += 1
    xo_ref = refs[pos]; pos += 1
    m = m_ref[...]
    xn = x_ref[...] + (m * _rms(m)) * gp_ref[...]
    xo_ref[...] = xn
    if h_mode is None:
        return
    h = (xn * _rms(xn)) * gn_ref[...]
    h_ref = refs[pos]; pos += 1
    if h_mode == "packed":
        packed = _pack_bf16_pairs(h)
        nb = packed.shape[1] // LANES
        for c in range(nb):
            h_ref[pl.ds(c, packed.shape[0], stride=nb), :] = packed[:, c * LANES:(c + 1) * LANES]
    elif h_mode == "pooled":
        _pool_tile(h, h_ref, refs[-1], tiles_per_seq)
    else:
        h_ref[...] = h.astype(h_ref.dtype)
    if router:
        refs[pos][...] = jnp.dot(h, wr_ref[...], preferred_element_type=F32,
                                 precision=lax.Precision.HIGHEST)


def resid_norm(x, m, g_post, g_next=None, h_mode=None, w_router=None, seq=None, tm=256):
    n, d = x.shape
    row = pl.BlockSpec((tm, d), lambda i: (i, 0))
    vec = pl.BlockSpec((1, d), lambda i: (0, 0))
    args = [x, m, g_post.reshape(1, d)]
    in_specs = [row, row, vec]
    out_shape = [jax.ShapeDtypeStruct((n, d), F32)]
    out_specs = [row]
    if h_mode is not None:
        args.append(g_next.reshape(1, d)); in_specs.append(vec)
        if h_mode == "packed":
            nb = d // 2 // LANES
            out_shape.append(jax.ShapeDtypeStruct((n * nb, LANES), U32))
            out_specs.append(pl.BlockSpec((tm * nb, LANES), lambda i: (i, 0)))
        else:
            out_shape.append(jax.ShapeDtypeStruct((n, d), BF16))
            out_specs.append(row)
    scratch = []
    if h_mode == "pooled":
        n_levels = len(POOL_WINDOWS)
        scratch.append(pltpu.VMEM((n_levels, 8, d), F32))
    router = w_router is not None
    if router:
        args.append(w_router)
        in_specs.append(pl.BlockSpec((d, LANES), lambda i: (0, 0)))
        out_shape.append(jax.ShapeDtypeStruct((n, LANES), F32))
        out_specs.append(pl.BlockSpec((tm, LANES), lambda i: (i, 0)))
    return pl.pallas_call(
        functools.partial(_resid_norm_kernel, h_mode=h_mode, router=router,
                          tiles_per_seq=None if seq is None else seq // tm),
        grid=(n // tm,),
        in_specs=in_specs, out_specs=out_specs, out_shape=out_shape,
        scratch_shapes=scratch,
        compiler_params=_cparams(1),
        name="resid_norm",
    )(*args)


N_SCHED = 9
KIND_NONE, KIND_COMPUTE, KIND_ZERO = 0, 1, 2


def _gmm_kernel(tile_ref, arow_ref, kind_ref, blk_ref, pfg_ref, pfw_ref, pfc_ref, pfd_ref,
                nblk_ref, *refs, n_a, n_w, n_col, has_scale, transposed, n_valid, merge_cast):
    del tile_ref, arow_ref, pfg_ref
    a_refs = refs[:n_a]
    c_refs = refs[n_a:n_a + n_w]
    pos = n_a + n_w
    s_ref = None
    if has_scale:
        s_ref = refs[pos]; pos += 1
    o_ref = refs[pos]; pos += 1
    slots = (refs[pos:pos + n_w], refs[pos + n_w:pos + 2 * n_w])
    jj = pl.program_id(0)
    s = pl.program_id(1)
    parity = (jj * nblk_ref[0] + blk_ref[s] + nblk_ref[0]) % 2
    rows_per_chunk = c_refs[0].shape[1]
    tcol = jj - 1 + pfw_ref[s]
    chunk_due = (pfd_ref[s] == 1) & (tcol >= 0) & (tcol < n_col)
    computing = (kind_ref[s] == KIND_COMPUTE) & (jj >= 1)

    def cast_chunk(nxt):
        r = pl.multiple_of(pfc_ref[s] * rows_per_chunk, rows_per_chunk)
        for c_ref, wbf in zip(c_refs, nxt):
            wbf[pl.ds(r, rows_per_chunk), :] = c_ref[0].astype(BF16)

    def matmul(wbf):
        if transposed:
            return lax.dot_general(a_refs[0][...], wbf[...], (((1,), (1,)), ((), ())),
                                   preferred_element_type=F32)
        acc = None
        off = 0
        for a_ref in a_refs:
            ka = a_ref.shape[1]
            part = jnp.dot(a_ref[...], wbf[off:off + ka, :], preferred_element_type=F32)
            acc = part if acc is None else acc + part
            off += ka
        return acc

    def compute(cur, nxt):
        if merge_cast:
            cast_chunk(nxt)
        acc = matmul(cur[0])
        if n_w == 2:
            acc = (acc * _sigmoid(acc)) * matmul(cur[1])
        if has_scale:
            acc = acc * s_ref[...]
        if n_valid is not None:
            col = lax.broadcasted_iota(I32, acc.shape, 1) + (jj - 1) * acc.shape[1]
            acc = jnp.where(col < n_valid, acc, 0.0)
        o_ref[...] = acc.astype(o_ref.dtype)

    for p in (0, 1):
        cur, nxt = slots[p], slots[1 - p]
        pl.when(computing & (parity == p))(functools.partial(compute, cur, nxt))
        cast_here = chunk_due & jnp.logical_not(computing) if merge_cast else chunk_due
        pl.when(cast_here & (parity == p))(functools.partial(cast_chunk, nxt))

    @pl.when((kind_ref[s] == KIND_ZERO) & (jj >= 1))
    def _():
        o_ref[...] = jnp.zeros_like(o_ref)


def _dense_schedule(n_tiles, nc):
    t = jnp.arange(n_tiles, dtype=I32)
    zeros = jnp.zeros((n_tiles,), I32)
    ones = jnp.ones((n_tiles,), I32)
    return (t, t, ones, zeros, zeros, ones, jnp.minimum(t, nc - 1), (t < nc).astype(I32),
            jnp.ones((1,), I32))


def gmm(a, ws, *, tm, tn, n_out, out_dtype, nc, sched=None, k=None, scale=None,
        a_col=None, w_group=None, w_col=None, transposed=False, n_valid=None,
        merge_cast=True):
    a_parts = list(a) if isinstance(a, (list, tuple)) else [a]
    rows = a_parts[0].shape[0]
    if k is None:
        k = sum(p.shape[1] for p in a_parts)
    if sched is None:
        sched = _dense_schedule(rows // tm, nc)
    n_steps = sched[0].shape[0]
    n_col = pl.cdiv(n_out, tn)
    a_col = a_col or (lambda j: 0)
    w_col = w_col or (lambda j: j)

    def cur_col(jj):
        return jnp.maximum(jj - 1, 0)

    def a_map(jj, s, tile, arow, *_):
        return (jnp.where(jj == 0, arow[0], arow[s]), a_col(cur_col(jj)))

    def o_map(jj, s, tile, *_):
        return (jnp.where(jj == 0, tile[0], tile[s]), cur_col(jj))

    def c_map(jj, s, tile, arow, kind, blk, pfg, pfw, pfc, *_):
        col = jnp.clip(jj - 1 + pfw[s], 0, n_col - 1)
        group = pfg[s] if w_group is None else w_group(col)
        if transposed:
            return (group, w_col(col) * nc + pfc[s], 0)
        return (group, pfc[s], w_col(col))

    in_specs = [pl.BlockSpec((tm, k if len(a_parts) == 1 else p.shape[1]), a_map)
                for p in a_parts]
    if transposed:
        assert len(a_parts) == 1 and tn % nc == 0
        chunk, wbf_shape = (1, tn // nc, k), (tn, k)
    else:
        assert k % nc == 0
        chunk, wbf_shape = (1, k // nc, tn), (k, tn)
    in_specs += [pl.BlockSpec(chunk, c_map) for _ in ws]
    args = [*a_parts, *ws]
    if scale is not None:
        in_specs.append(pl.BlockSpec((1, tn), lambda jj, s, *_: (0, cur_col(jj))))
        args.append(scale)
    grid_spec = pltpu.PrefetchScalarGridSpec(
        num_scalar_prefetch=N_SCHED,
        grid=(n_col + 1, n_steps),
        in_specs=in_specs,
        out_specs=pl.BlockSpec((tm, tn), o_map),
        scratch_shapes=[pltpu.VMEM(wbf_shape, BF16) for _ in range(2 * len(ws))],
    )
    return pl.pallas_call(
        functools.partial(_gmm_kernel, n_a=len(a_parts), n_w=len(ws), n_col=n_col,
                          has_scale=scale is not None, transposed=transposed,
                          n_valid=n_valid, merge_cast=merge_cast),
        grid_spec=grid_spec,
        out_shape=jax.ShapeDtypeStruct((rows, n_out), out_dtype),
        compiler_params=_cparams(2),
        name="gmm",
    )(*sched, *args)


def _forget_cumsum_kernel(f_ref, b_ref, o_ref):
    x = f_ref[...] + b_ref[...]
    ls = jnp.minimum(x, 0.0) - jnp.log1p(jnp.exp(-jnp.abs(x)))
    s = ls.shape[0]
    row = lax.broadcasted_iota(I32, ls.shape, 0)
    sh = 1
    while sh < s:
        ls = ls + jnp.where(row >= sh, pltpu.roll(ls, sh, 0), 0.0)
        sh *= 2
    o_ref[...] = ls


def forget_cumsum(f, b, batch):
    n, w = f.shape
    s = n // batch
    return pl.pallas_call(
        _forget_cumsum_kernel,
        grid=(batch,),
        in_specs=[pl.BlockSpec((s, w), lambda i: (i, 0)),
                  pl.BlockSpec((1, w), lambda i: (0, 0))],
        out_specs=pl.BlockSpec((s, w), lambda i: (i, 0)),
        out_shape=jax.ShapeDtypeStruct((n, w), F32),
        compiler_params=_cparams(1),
        name="forget_cumsum",
    )(f, b)


def _retention_kernel(lg_ref, q_ref, k_ref, v_ref, g_ref, cos_ref, sin_ref, o_ref):
    c = RET_CHUNK
    d = HEAD_DIM
    s = q_ref.shape[0]
    lg = lg_ref[pl.program_id(1)]
    ii = lax.broadcasted_iota(I32, (c, c), 0)
    jj = lax.broadcasted_iota(I32, (c, c), 1)
    diff = (ii - jj).astype(F32)
    decay_in = jnp.where(diff >= 0, jnp.exp(lg * jnp.maximum(diff, 0.0)), 0.0)
    jc = lax.broadcasted_iota(I32, (c, 1), 0).astype(F32)
    zeta = jnp.exp(lg * (c - 1 - jc))
    q_decay = jnp.exp(lg * (jc + 1.0))
    gamma_c = jnp.exp(jnp.full((1, 1), lg, F32) * c)
    scale = d ** -0.5
    nt = (((1,), (1,)), ((), ()))
    tn = (((0,), (0,)), ((), ()))
    state = jnp.zeros((d, d), F32)
    for n in range(s // c):
        sl = pl.ds(n * c, c)
        cos = cos_ref[sl, :]
        sin = sin_ref[sl, :]
        q = q_ref[sl, :].astype(F32)
        k = k_ref[sl, :].astype(F32)
        v = v_ref[sl, :]
        qr = q * cos + pltpu.roll(q, d // 2, 1) * sin
        kr = (k * cos + pltpu.roll(k, d // 2, 1) * sin) * scale
        scores = lax.dot_general(qr.astype(BF16), kr.astype(BF16), nt,
                                 preferred_element_type=F32) * decay_in
        inner = jnp.dot(scores.astype(BF16), v, preferred_element_type=F32)
        cross = jnp.dot((qr * q_decay).astype(BF16), state.astype(BF16),
                        preferred_element_type=F32)
        kv = lax.dot_general((kr * zeta).astype(BF16), v, tn, preferred_element_type=F32)
        state = gamma_c * state + kv
        o = inner + cross
        xc = o - jnp.mean(o, axis=-1, keepdims=True)
        y = xc * lax.rsqrt(jnp.mean(xc * xc, axis=-1, keepdims=True) + EPS)
        g = g_ref[sl, :].astype(F32)
        o_ref[sl, :] = ((g * _sigmoid(g)) * y).astype(o_ref.dtype)


def retention(proj, cos, sin, log_gamma, batch):
    n = proj.shape[0]
    s = n // batch
    d = HEAD_DIM
    hh = N_RET_HEADS

    def col(off):
        return pl.BlockSpec((s, d), lambda b, h, lg: (b, off + h))

    tab = pl.BlockSpec((s, d), lambda b, h, lg: (0, 0))
    grid_spec = pltpu.PrefetchScalarGridSpec(
        num_scalar_prefetch=1,
        grid=(batch, hh),
        in_specs=[col(0), col(hh), col(2 * hh), col(3 * hh), tab, tab],
        out_specs=pl.BlockSpec((s, d), lambda b, h, lg: (b, h)),
    )
    return pl.pallas_call(
        _retention_kernel,
        grid_spec=grid_spec,
        out_shape=jax.ShapeDtypeStruct((n, hh * d), BF16),
        compiler_params=_cparams(2),
        name="retention",
    )(log_gamma, proj, proj, proj, proj, cos, sin)


def _fox_kernel(q_ref, k_ref, v_ref, ccol_ref, crow_ref, o_ref, *, tq):
    s, d = q_ref.shape
    h = pl.program_id(1)
    scale = d ** -0.5
    nt = (((1,), (1,)), ((), ()))
    lane = lax.broadcasted_iota(I32, (tq, LANES), 1)
    rr = lax.broadcasted_iota(I32, (tq, tq), 0)
    cc = lax.broadcasted_iota(I32, (tq, tq), 1)

    for qi in range(s // tq):
        qs = pl.ds(qi * tq, tq)
        q = q_ref[qs, :]
        cq = jnp.sum(jnp.where(lane == h, ccol_ref[qs, :], 0.0), axis=-1, keepdims=True)

        def scores(kb):
            ks = pl.ds(pl.multiple_of(kb * tq, tq), tq)
            sc = lax.dot_general(q, k_ref[ks, :], nt, preferred_element_type=F32) * scale
            return sc + cq - crow_ref[:, ks], v_ref[ks, :]

        def update(carry, sc, v):
            m, l, acc = carry
            m_new = jnp.maximum(m, jnp.max(sc, axis=-1, keepdims=True))
            alpha = jnp.exp(m - m_new)
            p = jnp.exp(sc - m_new)
            l = alpha * l + jnp.sum(p, axis=-1, keepdims=True)
            acc = alpha * acc + jnp.dot(p.astype(BF16), v, preferred_element_type=F32)
            return m_new, l, acc

        def body(kb, carry):
            sc, v = scores(kb)
            return update(carry, sc, v)

        carry = (jnp.full((tq, 1), NEG_BIG, F32), jnp.zeros((tq, 1), F32),
                 jnp.zeros((tq, d), F32))
        carry = lax.fori_loop(0, qi, body, carry, unroll=True)
        sc, v = scores(qi)
        _, l, acc = update(carry, jnp.where(rr >= cc, sc, NEG_BIG), v)
        o_ref[qs, :] = (acc / l).astype(o_ref.dtype)


def fox_attention(proj, c_cols, c_rows, batch, tq=256):
    n = proj.shape[0]
    s = n // batch
    d = HEAD_DIM
    base = 4 * N_RET_HEADS
    hh = N_FOX_HEADS

    def col(off):
        return pl.BlockSpec((s, d), lambda b, h: (b, off + h))

    return pl.pallas_call(
        functools.partial(_fox_kernel, tq=tq),
        grid=(batch, hh),
        in_specs=[col(base), col(base + hh), col(base + 2 * hh),
                  pl.BlockSpec((s, LANES), lambda b, h: (b, 0)),
                  pl.BlockSpec((None, None, 1, s), lambda b, h: (b, h, 0, 0))],
        out_specs=pl.BlockSpec((s, d), lambda b, h: (b, h)),
        out_shape=jax.ShapeDtypeStruct((n, hh * d), BF16),
        compiler_params=_cparams(2),
        name="fox_attention",
    )(proj, proj, proj, c_cols, c_rows)


def _route_kernel(lg_ref, ids_ref, gate_ref, cnt_ref, carry_ref):
    tt = lg_ref.shape[0]

    @pl.when(pl.program_id(0) == 0)
    def _():
        carry_ref[...] = jnp.zeros_like(carry_ref)

    lane = lax.broadcasted_iota(I32, (tt, LANES), 1)
    lane_f = lane.astype(F32)
    x = jnp.where(lane < N_EXPERTS, lg_ref[...], -jnp.inf)
    v1 = jnp.max(x, axis=-1, keepdims=True)
    i1 = jnp.min(jnp.where(x == v1, lane_f, float(LANES)), axis=-1, keepdims=True)
    x2 = jnp.where(lane_f == i1, -jnp.inf, x)
    v2 = jnp.max(x2, axis=-1, keepdims=True)
    i2 = jnp.min(jnp.where(x2 == v2, lane_f, float(LANES)), axis=-1, keepdims=True)
    e = jnp.exp(v2 - v1)
    w1 = 1.0 / (1.0 + e)
    w2 = e / (1.0 + e)
    sel1 = lane_f == i1
    sel2 = lane_f == i2
    onehot = jnp.where(sel1, 1.0, jnp.where(sel2, 1.0, 0.0))
    rr = lax.broadcasted_iota(I32, (tt, tt), 0)
    cc = lax.broadcasted_iota(I32, (tt, tt), 1)
    tri = jnp.where(rr > cc, 1.0, 0.0).astype(BF16)
    before = jnp.dot(tri, onehot.astype(BF16), preferred_element_type=F32) + carry_ref[0:1, :]
    r1 = jnp.sum(jnp.where(sel1, before, 0.0), axis=-1, keepdims=True)
    r2 = jnp.sum(jnp.where(sel2, before, 0.0), axis=-1, keepdims=True)
    ids = jnp.where(lane == 0, i1, jnp.where(lane == 1, i2,
                    jnp.where(lane == 2, r1, jnp.where(lane == 3, r2, 0.0))))
    ids_ref[...] = ids.astype(I32)
    gate_ref[...] = jnp.where(lane == 0, w1, jnp.where(lane == 1, w2, 0.0))
    total = carry_ref[0:1, :] + jnp.sum(onehot, axis=0, keepdims=True)
    carry_ref[...] = jnp.broadcast_to(total, carry_ref.shape)
    cnt_ref[...] = jnp.broadcast_to(total, cnt_ref.shape).astype(I32)


def route(logits, tt=256):
    n = logits.shape[0]
    row = pl.BlockSpec((tt, LANES), lambda i: (i, 0))
    return pl.pallas_call(
        _route_kernel,
        grid=(n // tt,),
        in_specs=[row],
        out_specs=[row, row, pl.BlockSpec((8, LANES), lambda i: (0, 0))],
        out_shape=[jax.ShapeDtypeStruct((n, LANES), I32),
                   jax.ShapeDtypeStruct((n, LANES), F32),
                   jax.ShapeDtypeStruct((8, LANES), I32)],
        scratch_shapes=[pltpu.VMEM((8, LANES), F32)],
        compiler_params=_cparams(1),
        name="route",
    )(logits)


def _gather_kernel(pos1_ref, pos2_ref, h_hbm, o_ref, src_ref, buf_ref, sems, *,
                   n_tokens, nb):
    tg = o_ref.shape[0]
    i = pl.program_id(0)
    n_tiles = pl.num_programs(0) - 1

    @pl.when(i == 0)
    def _():
        def clear(p, carry):
            src_ref[p] = 0
            return carry
        lax.fori_loop(0, src_ref.shape[0], clear, 0, unroll=8)

        def place(t, carry):
            src_ref[pos1_ref[t]] = t
            src_ref[pos2_ref[t]] = t
            return carry
        lax.fori_loop(0, n_tokens, place, 0, unroll=4)

    def slab_copy(tile, slot, r):
        src = pl.multiple_of(src_ref[tile * tg + r] * nb, nb)
        dst = pl.multiple_of(r * nb, nb)
        return pltpu.make_async_copy(
            h_hbm.at[pl.ds(src, nb), :], buf_ref.at[slot, pl.ds(dst, nb), :], sems.at[slot])

    @pl.when(i < n_tiles)
    def _():
        def issue(r, carry):
            slab_copy(i, i % 2, r).start()
            return carry
        lax.fori_loop(0, tg, issue, 0, unroll=4)

    @pl.when(i > 0)
    def _():
        slot = (i - 1) % 2

        def drain(r, carry):
            slab_copy(i - 1, slot, r).wait()
            return carry
        lax.fori_loop(0, tg, drain, 0, unroll=4)
        half = nb * LANES
        for c in range(nb):
            lo, hi = _unpack_bf16_pairs(buf_ref[slot, pl.ds(c, tg, stride=nb), :])
            o_ref[:, c * LANES:(c + 1) * LANES] = lo
            o_ref[:, half + c * LANES:half + (c + 1) * LANES] = hi


def gather_rows(h_packed, pos1, pos2, n_rows, d, tg=256):
    nb = d // 2 // LANES
    n = h_packed.shape[0] // nb
    width = 2 * nb * LANES
    n_tiles = n_rows // tg
    grid_spec = pltpu.PrefetchScalarGridSpec(
        num_scalar_prefetch=2,
        grid=(n_tiles + 1,),
        in_specs=[pl.BlockSpec(memory_space=pl.ANY)],
        out_specs=pl.BlockSpec((tg, width), lambda i, p1, p2: (jnp.maximum(i - 1, 0), 0)),
        scratch_shapes=[pltpu.SMEM((n_rows,), I32), pltpu.VMEM((2, tg * nb, LANES), U32),
                        pltpu.SemaphoreType.DMA((2,))],
    )
    return pl.pallas_call(
        functools.partial(_gather_kernel, n_tokens=n, nb=nb),
        grid_spec=grid_spec,
        out_shape=jax.ShapeDtypeStruct((n_rows, width), BF16),
        compiler_params=_cparams(1),
        name="gather_rows",
    )(pos1, pos2, h_packed)


def _combine_kernel(pos1_ref, pos2_ref, o_hbm, gate_ref, x_ref, g_ref, out_ref, buf_ref, sems):
    tt = x_ref.shape[0]
    i = pl.program_id(0)
    n_tiles = pl.num_programs(0) - 1

    def row_copy(tile, slot, r, which, pos_ref):
        return pltpu.make_async_copy(
            o_hbm.at[pl.ds(pos_ref[tile * tt + r], 1), :],
            buf_ref.at[slot, which, pl.ds(r, 1), :], sems.at[slot])

    @pl.when(i < n_tiles)
    def _():
        def issue(r, carry):
            row_copy(i, i % 2, r, 0, pos1_ref).start()
            row_copy(i, i % 2, r, 1, pos2_ref).start()
            return carry
        lax.fori_loop(0, tt, issue, 0, unroll=4)

    @pl.when(i > 0)
    def _():
        slot = (i - 1) % 2

        def drain(r, carry):
            row_copy(i - 1, slot, r, 0, pos1_ref).wait()
            row_copy(i - 1, slot, r, 1, pos2_ref).wait()
            return carry
        lax.fori_loop(0, tt, drain, 0, unroll=4)
        gate = gate_ref[...]
        y = gate[:, 0:1] * buf_ref[slot, 0] + gate[:, 1:2] * buf_ref[slot, 1]
        out_ref[...] = x_ref[...] + (y * _rms(y)) * g_ref[...]


def combine(o, pos1, pos2, gate, x, g, tt=128):
    n, d = x.shape
    row = lambda i, p1, p2: (jnp.maximum(i - 1, 0), 0)
    grid_spec = pltpu.PrefetchScalarGridSpec(
        num_scalar_prefetch=2,
        grid=(n // tt + 1,),
        in_specs=[pl.BlockSpec(memory_space=pl.ANY),
                  pl.BlockSpec((tt, LANES), row),
                  pl.BlockSpec((tt, d), row),
                  pl.BlockSpec((1, d), lambda i, p1, p2: (0, 0))],
        out_specs=pl.BlockSpec((tt, d), row),
        scratch_shapes=[pltpu.VMEM((2, 2, tt, d), F32), pltpu.SemaphoreType.DMA((2,))],
    )
    return pl.pallas_call(
        _combine_kernel,
        grid_spec=grid_spec,
        out_shape=jax.ShapeDtypeStruct((n, d), F32),
        compiler_params=_cparams(1),
        name="combine",
    )(pos1, pos2, o, gate, x, g.reshape(1, d))


def _rope_tables(s):
    half = HEAD_DIM // 2
    inv = ROPE_BASE ** (-jnp.arange(half, dtype=F32) / half)
    ang = jnp.arange(s, dtype=F32)[:, None] * inv[None, :]
    cos, sin = jnp.cos(ang), jnp.sin(ang)
    return jnp.concatenate([cos, cos], axis=-1), jnp.concatenate([-sin, sin], axis=-1)


def _moe_schedule(ids, counts, tm, n_tiles, nc):
    ne = N_EXPERTS
    counts = counts[0, :ne]
    tiles = (counts + tm - 1) // tm
    nonempty = tiles > 0
    tile_end = jnp.cumsum(tiles)
    tile_start = tile_end - tiles
    pos1 = tile_start[ids[:, 0]] * tm + ids[:, 2]
    pos2 = tile_start[ids[:, 1]] * tm + ids[:, 3]

    steps = jnp.where(nonempty, jnp.maximum(tiles, nc), 0)
    step_end = jnp.cumsum(steps)
    step_start = step_end - steps
    total, n_used = step_end[-1], tile_end[-1]
    ar = jnp.arange(ne, dtype=I32)
    first_e = jnp.min(jnp.where(nonempty, ar, ne))
    last_e = jnp.max(jnp.where(nonempty, ar, -1))
    later = (ar[None, :] > ar[:, None]) & nonempty[None, :]
    nxt = jnp.min(jnp.where(later, ar[None, :], ne), axis=1)
    wrap = nxt == ne
    nxt = jnp.where(wrap, first_e, nxt)
    rank = jnp.cumsum(nonempty.astype(I32)) - 1

    s = jnp.arange(n_tiles + ne * (nc - 1), dtype=I32)
    in_blocks = s < total
    e = jnp.minimum(jnp.sum((s[:, None] >= step_end[None, :]).astype(I32), axis=1), ne - 1)
    e = jnp.where(in_blocks, e, last_e)
    p = s - step_start[e]
    is_comp = in_blocks & (p < tiles[e])
    z = s - total
    is_zero = (~in_blocks) & (z < n_tiles - n_used)
    group_last = tile_start[e] + tiles[e] - 1
    tile = jnp.where(is_comp, tile_start[e] + p,
                     jnp.where(is_zero, n_used + z, jnp.where(in_blocks, group_last, n_tiles - 1)))
    arow = jnp.where(is_comp, tile, jnp.where(in_blocks, group_last, n_used - 1))
    kind = jnp.where(is_comp, KIND_COMPUTE, jnp.where(is_zero, KIND_ZERO, KIND_NONE))
    pf_chunk = jnp.where(in_blocks, jnp.minimum(p, nc - 1), nc - 1)
    pf_do = in_blocks & (p < nc)
    sched = (tile, arow, kind, rank[e], nxt[e], wrap[e], pf_chunk, pf_do,
             jnp.sum(nonempty.astype(I32)).reshape(1))
    return pos1.astype(I32), pos2.astype(I32), tuple(v.astype(I32) for v in sched)


def kernel(x, even_norm_mix_pre, even_w_in, even_b_forget, even_w_out, even_norm_mix_post,
           even_norm_ffn_pre, even_w_gate, even_w_up, even_w_down, even_norm_ffn_post,
           odd_norm_mix_pre, odd_w_pool, odd_pool_scale, odd_norm_mix_post, odd_norm_ffn_pre,
           odd_w_router, odd_we_gate, odd_we_up, odd_we_down, odd_norm_ffn_post):
    batch, seq, d = x.shape
    n = batch * seq
    x0 = x.reshape(n, d)
    ret_w = N_RET_HEADS * HEAD_DIM
    fox_w = N_FOX_HEADS * HEAD_DIM
    main_cols = 4 * ret_w + 3 * fox_w

    h0 = rms_cast(x0, even_norm_mix_pre[0])
    w_in_t = jnp.swapaxes(even_w_in, 1, 2)
    proj = gmm(h0, [w_in_t], tm=1024, tn=1024, n_out=main_cols, out_dtype=BF16, nc=4,
               transposed=True)
    b_forget = jnp.pad(even_b_forget[0], (0, LANES - N_FOX_HEADS)).reshape(1, LANES)
    f_logit = gmm(h0, [w_in_t], tm=512, tn=LANES, n_out=LANES, out_dtype=F32, nc=1,
                  transposed=True, w_col=lambda j: main_cols // LANES + j,
                  n_valid=N_FOX_HEADS)
    c_cols = forget_cumsum(f_logit, b_forget, batch)
    c_rows = c_cols[:, :N_FOX_HEADS].reshape(batch, seq, N_FOX_HEADS)
    c_rows = c_rows.transpose(0, 2, 1).reshape(batch, N_FOX_HEADS, 1, seq)
    cos, sin = _rope_tables(seq)
    log_gamma = jnp.log1p(-(2.0 ** (-5.0 - jnp.arange(N_RET_HEADS, dtype=F32))))
    ret = retention(proj, cos, sin, log_gamma, batch)
    fox = fox_attention(proj, c_cols, c_rows, batch)
    m = gmm([ret, fox], [even_w_out], tm=1024, tn=1024, n_out=d, out_dtype=F32, nc=8)
    x1, h1 = resid_norm(x0, m, even_norm_mix_post[0], even_norm_ffn_pre[0], "bf16")

    d_ff = even_w_gate.shape[-1]
    act = gmm(h1, [even_w_gate, even_w_up], tm=1024, tn=512, n_out=d_ff, out_dtype=BF16, nc=4)
    f = gmm(act, [even_w_down], tm=512, tn=512, n_out=d, out_dtype=F32, nc=16)
    x2, pooled = resid_norm(x1, f, even_norm_ffn_post[0], odd_norm_mix_pre[0], "pooled",
                            seq=seq)
    group = d // len(POOL_WINDOWS)
    tn_pool = 512
    per = group // tn_pool
    m2 = gmm(pooled, [odd_w_pool[0]], tm=2048, tn=tn_pool, n_out=d, out_dtype=F32, nc=2,
             k=group, scale=odd_pool_scale[0].reshape(1, d),
             a_col=lambda j: j // per, w_group=lambda j: j // per, w_col=lambda j: j % per)
    w_router = jnp.pad(odd_w_router[0], ((0, 0), (0, LANES - N_EXPERTS)))
    x3, h3, logits = resid_norm(x2, m2, odd_norm_mix_post[0], odd_norm_ffn_pre[0], "packed",
                                w_router=w_router)

    ids, gate, counts = route(logits)
    tm = MOE_TILE
    n_rows = 2 * n + N_EXPERTS * tm
    pos1, pos2, sched = _moe_schedule(ids, counts, tm, n_rows // tm, MOE_CHUNKS)
    xs = gather_rows(h3, pos1, pos2, n_rows, d)
    d_fe = odd_we_gate.shape[-1]
    act2 = gmm(xs, [odd_we_gate[0], odd_we_up[0]], tm=tm, tn=1024, n_out=d_fe,
               out_dtype=BF16, nc=MOE_CHUNKS, sched=sched, merge_cast=False)
    o = gmm(act2, [odd_we_down[0]], tm=tm, tn=1024, n_out=d, out_dtype=F32,
            nc=MOE_CHUNKS, sched=sched, merge_cast=False)
    out = combine(o, pos1, pos2, gate, x3, odd_norm_ffn_post[0])
    return out.reshape(batch, seq, d)
```

```python
import functools

import jax
import jax.numpy as jnp
from jax import lax
from jax.experimental import pallas as pl
from jax.experimental.pallas import tpu as pltpu

F32 = jnp.float32
BF16 = jnp.bfloat16
U32 = jnp.uint32
I32 = jnp.int32

HEAD_DIM = 128
N_RET_HEADS = 16
N_FOX_HEADS = 16
RET_CHUNK = 128
ROPE_BASE = 10000.0
N_EXPERTS = 8
POOL_WINDOWS = (2, 4, 8, 16)
EPS = 1e-6
LANES = 128
NEG_BIG = -1e30

VMEM_LIMIT = 56 * 1024 * 1024
MOE_TILE = 256
MOE_CHUNKS = 8


def _cparams(n_axes):
    return pltpu.CompilerParams(
        dimension_semantics=("arbitrary",) * n_axes, vmem_limit_bytes=VMEM_LIMIT)


def _rms(v):
    return lax.rsqrt(jnp.mean(v * v, axis=-1, keepdims=True) + EPS)


def _sigmoid(v):
    return 1.0 / (1.0 + jnp.exp(-v))


def _rms_cast_kernel(x_ref, g_ref, o_ref):
    x = x_ref[...]
    o_ref[...] = ((x * _rms(x)) * g_ref[...]).astype(o_ref.dtype)


def rms_cast(x, g, tm=256):
    n, d = x.shape
    return pl.pallas_call(
        _rms_cast_kernel,
        grid=(n // tm,),
        in_specs=[pl.BlockSpec((tm, d), lambda i: (i, 0)),
                  pl.BlockSpec((1, d), lambda i: (0, 0))],
        out_specs=pl.BlockSpec((tm, d), lambda i: (i, 0)),
        out_shape=jax.ShapeDtypeStruct((n, d), BF16),
        compiler_params=_cparams(1),
        name="rms_cast",
    )(x, g.reshape(1, d))


def _pack_bf16_pairs(h):
    half = h.shape[1] // 2
    bits = lax.bitcast_convert_type(h.astype(BF16).astype(F32), U32)
    return (bits[:, :half] >> 16) | (bits[:, half:] & jnp.uint32(0xFFFF0000))


def _unpack_bf16_pairs(p):
    lo = lax.bitcast_convert_type(p << 16, F32).astype(BF16)
    hi = lax.bitcast_convert_type(p & jnp.uint32(0xFFFF0000), F32).astype(BF16)
    return lo, hi


def _pool_tile(h, o_ref, halo_ref, tiles_per_seq):
    tm, d = h.shape
    group = d // len(POOL_WINDOWS)
    seq_tile = pl.program_id(0) % tiles_per_seq

    @pl.when(seq_tile == 0)
    def _():
        halo_ref[...] = jnp.zeros_like(halo_ref)

    row = lax.broadcasted_iota(I32, (tm, group), 0)
    row8 = lax.broadcasted_iota(I32, (8, group), 0)
    count_cap = seq_tile * tm + row + 1
    for gi, w in enumerate(POOL_WINDOWS):
        cols = slice(gi * group, (gi + 1) * group)
        x = h[:, cols]
        acc = x
        by, level = 1, 0
        while by < w:
            prev = halo_ref[level, :, cols]
            rolled = pltpu.roll(acc, by, 0)
            prev_rolled = pltpu.roll(prev, by, 0) if by < 8 else prev
            top = jnp.where(row8 < by, prev_rolled, rolled[0:8])
            halo_ref[level, :, cols] = acc[tm - 8:tm]
            acc = acc + jnp.concatenate([top, rolled[8:]], axis=0)
            by *= 2
            level += 1
        count = jnp.minimum(count_cap, w).astype(F32)
        o_ref[:, cols] = (acc / count - x).astype(o_ref.dtype)


def _resid_norm_kernel(*refs, h_mode, router, tiles_per_seq):
    x_ref, m_ref, gp_ref = refs[:3]
    pos = 3
    gn_ref = wr_ref = None
    if h_mode is not None:
        gn_ref = refs[pos]; pos += 1
    if router:
        wr_ref = refs[pos]; pos += 1
    xo_ref = refs[pos]; pos += 1
    m = m_ref[...]
    xn = x_ref[...] + (m * _rms(m)) * gp_ref[...]
    xo_ref[...] = xn
    if h_mode is None:
        return
    h = (xn * _rms(xn)) * gn_ref[...]
    h_ref = refs[pos]; pos += 1
    if h_mode == "packed":
        packed = _pack_bf16_pairs(h)
        nb = packed.shape[1] // LANES
        for c in range(nb):
            h_ref[pl.ds(c, packed.shape[0], stride=nb), :] = packed[:, c * LANES:(c + 1) * LANES]
    elif h_mode == "pooled":
        _pool_tile(h, h_ref, refs[-1], tiles_per_seq)
    else:
        h_ref[...] = h.astype(h_ref.dtype)
    if router:
        refs[pos][...] = jnp.dot(h, wr_ref[...], preferred_element_type=F32,
                                 precision=lax.Precision.HIGHEST)


def resid_norm(x, m, g_post, g_next=None, h_mode=None, w_router=None, seq=None, tm=256):
    n, d = x.shape
    row = pl.BlockSpec((tm, d), lambda i: (i, 0))
    vec = pl.BlockSpec((1, d), lambda i: (0, 0))
    args = [x, m, g_post.reshape(1, d)]
    in_specs = [row, row, vec]
    out_shape = [jax.ShapeDtypeStruct((n, d), F32)]
    out_specs = [row]
    if h_mode is not None:
        args.append(g_next.reshape(1, d)); in_specs.append(vec)
        if h_mode == "packed":
            nb = d // 2 // LANES
            out_shape.append(jax.ShapeDtypeStruct((n * nb, LANES), U32))
            out_specs.append(pl.BlockSpec((tm * nb, LANES), lambda i: (i, 0)))
        else:
            out_shape.append(jax.ShapeDtypeStruct((n, d), BF16))
            out_specs.append(row)
    scratch = []
    if h_mode == "pooled":
        n_levels = len(POOL_WINDOWS)
        scratch.append(pltpu.VMEM((n_levels, 8, d), F32))
    router = w_router is not None
    if router:
        args.append(w_router)
        in_specs.append(pl.BlockSpec((d, LANES), lambda i: (0, 0)))
        out_shape.append(jax.ShapeDtypeStruct((n, LANES), F32))
        out_specs.append(pl.BlockSpec((tm, LANES), lambda i: (i, 0)))
    return pl.pallas_call(
        functools.partial(_resid_norm_kernel, h_mode=h_mode, router=router,
                          tiles_per_seq=None if seq is None else seq // tm),
        grid=(n // tm,),
        in_specs=in_specs, out_specs=out_specs, out_shape=out_shape,
        scratch_shapes=scratch,
        compiler_params=_cparams(1),
        name="resid_norm",
    )(*args)


N_SCHED = 9
KIND_NONE, KIND_COMPUTE, KIND_ZERO = 0, 1, 2


def _gmm_kernel(tile_ref, arow_ref, kind_ref, blk_ref, pfg_ref, pfw_ref, pfc_ref, pfd_ref,
                nblk_ref, *refs, n_a, n_w, n_col, has_scale, transposed, n_valid, merge_cast):
    del tile_ref, arow_ref, pfg_ref
    a_refs = refs[:n_a]
    c_refs = refs[n_a:n_a + n_w]
    pos = n_a + n_w
    s_ref = None
    if has_scale:
        s_ref = refs[pos]; pos += 1
    o_ref = refs[pos]; pos += 1
    slots = (refs[pos:pos + n_w], refs[pos + n_w:pos + 2 * n_w])
    jj = pl.program_id(0)
    s = pl.program_id(1)
    parity = (jj * nblk_ref[0] + blk_ref[s] + nblk_ref[0]) % 2
    rows_per_chunk = c_refs[0].shape[1]
    tcol = jj - 1 + pfw_ref[s]
    chunk_due = (pfd_ref[s] == 1) & (tcol >= 0) & (tcol < n_col)
    computing = (kind_ref[s] == KIND_COMPUTE) & (jj >= 1)

    def cast_chunk(nxt):
        r = pl.multiple_of(pfc_ref[s] * rows_per_chunk, rows_per_chunk)
        for c_ref, wbf in zip(c_refs, nxt):
            wbf[pl.ds(r, rows_per_chunk), :] = c_ref[0].astype(BF16)

    def matmul(wbf):
        if transposed:
            return lax.dot_general(a_refs[0][...], wbf[...], (((1,), (1,)), ((), ())),
                                   preferred_element_type=F32)
        acc = None
        off = 0
        for a_ref in a_refs:
            ka = a_ref.shape[1]
            part = jnp.dot(a_ref[...], wbf[off:off + ka, :], preferred_element_type=F32)
            acc = part if acc is None else acc + part
            off += ka
        return acc

    def compute(cur, nxt):
        if merge_cast:
            cast_chunk(nxt)
        acc = matmul(cur[0])
        if n_w == 2:
            acc = (acc * _sigmoid(acc)) * matmul(cur[1])
        if has_scale:
            acc = acc * s_ref[...]
        if n_valid is not None:
            col = lax.broadcasted_iota(I32, acc.shape, 1) + (jj - 1) * acc.shape[1]
            acc = jnp.where(col < n_valid, acc, 0.0)
        o_ref[...] = acc.astype(o_ref.dtype)

    for p in (0, 1):
        cur, nxt = slots[p], slots[1 - p]
        pl.when(computing & (parity == p))(functools.partial(compute, cur, nxt))
        cast_here = chunk_due & jnp.logical_not(computing) if merge_cast else chunk_due
        pl.when(cast_here & (parity == p))(functools.partial(cast_chunk, nxt))

    @pl.when((kind_ref[s] == KIND_ZERO) & (jj >= 1))
    def _():
        o_ref[...] = jnp.zeros_like(o_ref)


def _dense_schedule(n_tiles, nc):
    t = jnp.arange(n_tiles, dtype=I32)
    zeros = jnp.zeros((n_tiles,), I32)
    ones = jnp.ones((n_tiles,), I32)
    return (t, t, ones, zeros, zeros, ones, jnp.minimum(t, nc - 1), (t < nc).astype(I32),
            jnp.ones((1,), I32))


def gmm(a, ws, *, tm, tn, n_out, out_dtype, nc, sched=None, k=None, scale=None,
        a_col=None, w_group=None, w_col=None, transposed=False, n_valid=None,
        merge_cast=True, n_steps=None):
    a_parts = list(a) if isinstance(a, (list, tuple)) else [a]
    rows = a_parts[0].shape[0]
    if k is None:
        k = sum(p.shape[1] for p in a_parts)
    if sched is None:
        sched = _dense_schedule(rows // tm, nc)
    if n_steps is None:
        n_steps = sched[0].shape[0]
    n_col = pl.cdiv(n_out, tn)
    a_col = a_col or (lambda j: 0)
    w_col = w_col or (lambda j: j)

    def cur_col(jj):
        return jnp.maximum(jj - 1, 0)

    def a_map(jj, s, tile, arow, *_):
        return (jnp.where(jj == 0, arow[0], arow[s]), a_col(cur_col(jj)))

    def o_map(jj, s, tile, *_):
        return (jnp.where(jj == 0, tile[0], tile[s]), cur_col(jj))

    def c_map(jj, s, tile, arow, kind, blk, pfg, pfw, pfc, *_):
        col = jnp.clip(jj - 1 + pfw[s], 0, n_col - 1)
        group = pfg[s] if w_group is None else w_group(col)
        if transposed:
            return (group, w_col(col) * nc + pfc[s], 0)
        return (group, pfc[s], w_col(col))

    in_specs = [pl.BlockSpec((tm, k if len(a_parts) == 1 else p.shape[1]), a_map)
                for p in a_parts]
    if transposed:
        assert len(a_parts) == 1 and tn % nc == 0
        chunk, wbf_shape = (1, tn // nc, k), (tn, k)
    else:
        assert k % nc == 0
        chunk, wbf_shape = (1, k // nc, tn), (k, tn)
    in_specs += [pl.BlockSpec(chunk, c_map) for _ in ws]
    args = [*a_parts, *ws]
    if scale is not None:
        in_specs.append(pl.BlockSpec((1, tn), lambda jj, s, *_: (0, cur_col(jj))))
        args.append(scale)
    grid_spec = pltpu.PrefetchScalarGridSpec(
        num_scalar_prefetch=N_SCHED,
        grid=(n_col + 1, n_steps),
        in_specs=in_specs,
        out_specs=pl.BlockSpec((tm, tn), o_map),
        scratch_shapes=[pltpu.VMEM(wbf_shape, BF16) for _ in range(2 * len(ws))],
    )
    return pl.pallas_call(
        functools.partial(_gmm_kernel, n_a=len(a_parts), n_w=len(ws), n_col=n_col,
                          has_scale=scale is not None, transposed=transposed,
                          n_valid=n_valid, merge_cast=merge_cast),
        grid_spec=grid_spec,
        out_shape=jax.ShapeDtypeStruct((rows, n_out), out_dtype),
        compiler_params=_cparams(2),
        name="gmm",
    )(*sched, *args)


def _forget_cumsum_kernel(f_ref, b_ref, o_ref):
    x = f_ref[...] + b_ref[...]
    ls = jnp.minimum(x, 0.0) - jnp.log1p(jnp.exp(-jnp.abs(x)))
    s = ls.shape[0]
    row = lax.broadcasted_iota(I32, ls.shape, 0)
    sh = 1
    while sh < s:
        ls = ls + jnp.where(row >= sh, pltpu.roll(ls, sh, 0), 0.0)
        sh *= 2
    o_ref[...] = ls


def forget_cumsum(f, b, batch):
    n, w = f.shape
    s = n // batch
    return pl.pallas_call(
        _forget_cumsum_kernel,
        grid=(batch,),
        in_specs=[pl.BlockSpec((s, w), lambda i: (i, 0)),
                  pl.BlockSpec((1, w), lambda i: (0, 0))],
        out_specs=pl.BlockSpec((s, w), lambda i: (i, 0)),
        out_shape=jax.ShapeDtypeStruct((n, w), F32),
        compiler_params=_cparams(1),
        name="forget_cumsum",
    )(f, b)


def _retention_kernel(lg_ref, q_ref, k_ref, v_ref, g_ref, cos_ref, sin_ref, o_ref):
    c = RET_CHUNK
    d = HEAD_DIM
    s = q_ref.shape[0]
    lg = lg_ref[pl.program_id(1)]
    ii = lax.broadcasted_iota(I32, (c, c), 0)
    jj = lax.broadcasted_iota(I32, (c, c), 1)
    diff = (ii - jj).astype(F32)
    decay_in = jnp.where(diff >= 0, jnp.exp(lg * jnp.maximum(diff, 0.0)), 0.0)
    jc = lax.broadcasted_iota(I32, (c, 1), 0).astype(F32)
    zeta = jnp.exp(lg * (c - 1 - jc))
    q_decay = jnp.exp(lg * (jc + 1.0))
    gamma_c = jnp.exp(jnp.full((1, 1), lg, F32) * c)
    scale = d ** -0.5
    nt = (((1,), (1,)), ((), ()))
    tn = (((0,), (0,)), ((), ()))
    state = jnp.zeros((d, d), F32)
    for n in range(s // c):
        sl = pl.ds(n * c, c)
        cos = cos_ref[sl, :]
        sin = sin_ref[sl, :]
        q = q_ref[sl, :].astype(F32)
        k = k_ref[sl, :].astype(F32)
        v = v_ref[sl, :]
        qr = q * cos + pltpu.roll(q, d // 2, 1) * sin
        kr = (k * cos + pltpu.roll(k, d // 2, 1) * sin) * scale
        scores = lax.dot_general(qr.astype(BF16), kr.astype(BF16), nt,
                                 preferred_element_type=F32) * decay_in
        inner = jnp.dot(scores.astype(BF16), v, preferred_element_type=F32)
        cross = jnp.dot((qr * q_decay).astype(BF16), state.astype(BF16),
                        preferred_element_type=F32)
        kv = lax.dot_general((kr * zeta).astype(BF16), v, tn, preferred_element_type=F32)
        state = gamma_c * state + kv
        o = inner + cross
        xc = o - jnp.mean(o, axis=-1, keepdims=True)
        y = xc * lax.rsqrt(jnp.mean(xc * xc, axis=-1, keepdims=True) + EPS)
        g = g_ref[sl, :].astype(F32)
        o_ref[sl, :] = ((g * _sigmoid(g)) * y).astype(o_ref.dtype)


def retention(proj, cos, sin, log_gamma, batch):
    n = proj.shape[0]
    s = n // batch
    d = HEAD_DIM
    hh = N_RET_HEADS

    def col(off):
        return pl.BlockSpec((s, d), lambda b, h, lg: (b, off + h))

    tab = pl.BlockSpec((s, d), lambda b, h, lg: (0, 0))
    grid_spec = pltpu.PrefetchScalarGridSpec(
        num_scalar_prefetch=1,
        grid=(batch, hh),
        in_specs=[col(0), col(hh), col(2 * hh), col(3 * hh), tab, tab],
        out_specs=pl.BlockSpec((s, d), lambda b, h, lg: (b, h)),
    )
    return pl.pallas_call(
        _retention_kernel,
        grid_spec=grid_spec,
        out_shape=jax.ShapeDtypeStruct((n, hh * d), BF16),
        compiler_params=_cparams(2),
        name="retention",
    )(log_gamma, proj, proj, proj, proj, cos, sin)


def _fox_kernel(q_ref, k_ref, v_ref, ccol_ref, crow_ref, o_ref, *, tq):
    s, d = q_ref.shape
    h = pl.program_id(1)
    scale = d ** -0.5
    nt = (((1,), (1,)), ((), ()))
    lane = lax.broadcasted_iota(I32, (tq, LANES), 1)
    rr = lax.broadcasted_iota(I32, (tq, tq), 0)
    cc = lax.broadcasted_iota(I32, (tq, tq), 1)

    for qi in range(s // tq):
        qs = pl.ds(qi * tq, tq)
        q = q_ref[qs, :]
        cq = jnp.sum(jnp.where(lane == h, ccol_ref[qs, :], 0.0), axis=-1, keepdims=True)

        def scores(kb):
            ks = pl.ds(pl.multiple_of(kb * tq, tq), tq)
            sc = lax.dot_general(q, k_ref[ks, :], nt, preferred_element_type=F32) * scale
            return sc + cq - crow_ref[:, ks], v_ref[ks, :]

        def update(carry, sc, v):
            m, l, acc = carry
            m_new = jnp.maximum(m, jnp.max(sc, axis=-1, keepdims=True))
            alpha = jnp.exp(m - m_new)
            p = jnp.exp(sc - m_new)
            l = alpha * l + jnp.sum(p, axis=-1, keepdims=True)
            acc = alpha * acc + jnp.dot(p.astype(BF16), v, preferred_element_type=F32)
            return m_new, l, acc

        def body(kb, carry):
            sc, v = scores(kb)
            return update(carry, sc, v)

        carry = (jnp.full((tq, 1), NEG_BIG, F32), jnp.zeros((tq, 1), F32),
                 jnp.zeros((tq, d), F32))
        carry = lax.fori_loop(0, qi, body, carry, unroll=True)
        sc, v = scores(qi)
        _, l, acc = update(carry, jnp.where(rr >= cc, sc, NEG_BIG), v)
        o_ref[qs, :] = (acc / l).astype(o_ref.dtype)


def fox_attention(proj, c_cols, c_rows, batch, tq=256):
    n = proj.shape[0]
    s = n // batch
    d = HEAD_DIM
    base = 4 * N_RET_HEADS
    hh = N_FOX_HEADS

    def col(off):
        return pl.BlockSpec((s, d), lambda b, h: (b, off + h))

    return pl.pallas_call(
        functools.partial(_fox_kernel, tq=tq),
        grid=(batch, hh),
        in_specs=[col(base), col(base + hh), col(base + 2 * hh),
                  pl.BlockSpec((s, LANES), lambda b, h: (b, 0)),
                  pl.BlockSpec((None, None, 1, s), lambda b, h: (b, h, 0, 0))],
        out_specs=pl.BlockSpec((s, d), lambda b, h: (b, h)),
        out_shape=jax.ShapeDtypeStruct((n, hh * d), BF16),
        compiler_params=_cparams(2),
        name="fox_attention",
    )(proj, proj, proj, c_cols, c_rows)


def _route_kernel(lg_ref, ids_ref, gate_ref, cnt_ref, carry_ref):
    tt = lg_ref.shape[0]

    @pl.when(pl.program_id(0) == 0)
    def _():
        carry_ref[...] = jnp.zeros_like(carry_ref)

    lane = lax.broadcasted_iota(I32, (tt, LANES), 1)
    lane_f = lane.astype(F32)
    x = jnp.where(lane < N_EXPERTS, lg_ref[...], -jnp.inf)
    v1 = jnp.max(x, axis=-1, keepdims=True)
    i1 = jnp.min(jnp.where(x == v1, lane_f, float(LANES)), axis=-1, keepdims=True)
    x2 = jnp.where(lane_f == i1, -jnp.inf, x)
    v2 = jnp.max(x2, axis=-1, keepdims=True)
    i2 = jnp.min(jnp.where(x2 == v2, lane_f, float(LANES)), axis=-1, keepdims=True)
    e = jnp.exp(v2 - v1)
    w1 = 1.0 / (1.0 + e)
    w2 = e / (1.0 + e)
    sel1 = lane_f == i1
    sel2 = lane_f == i2
    onehot = jnp.where(sel1, 1.0, jnp.where(sel2, 1.0, 0.0))
    rr = lax.broadcasted_iota(I32, (tt, tt), 0)
    cc = lax.broadcasted_iota(I32, (tt, tt), 1)
    tri = jnp.where(rr > cc, 1.0, 0.0).astype(BF16)
    before = jnp.dot(tri, onehot.astype(BF16), preferred_element_type=F32) + carry_ref[0:1, :]
    r1 = jnp.sum(jnp.where(sel1, before, 0.0), axis=-1, keepdims=True)
    r2 = jnp.sum(jnp.where(sel2, before, 0.0), axis=-1, keepdims=True)
    ids = jnp.where(lane == 0, i1, jnp.where(lane == 1, i2,
                    jnp.where(lane == 2, r1, jnp.where(lane == 3, r2, 0.0))))
    ids_ref[...] = ids.astype(I32)
    gate_ref[...] = jnp.where(lane == 0, w1, jnp.where(lane == 1, w2, 0.0))
    total = carry_ref[0:1, :] + jnp.sum(onehot, axis=0, keepdims=True)
    carry_ref[...] = jnp.broadcast_to(total, carry_ref.shape)
    cnt_ref[...] = jnp.broadcast_to(total, cnt_ref.shape).astype(I32)


def route(logits, tt=256):
    n = logits.shape[0]
    row = pl.BlockSpec((tt, LANES), lambda i: (i, 0))
    return pl.pallas_call(
        _route_kernel,
        grid=(n // tt,),
        in_specs=[row],
        out_specs=[row, row, pl.BlockSpec((8, LANES), lambda i: (0, 0))],
        out_shape=[jax.ShapeDtypeStruct((n, LANES), I32),
                   jax.ShapeDtypeStruct((n, LANES), F32),
                   jax.ShapeDtypeStruct((8, LANES), I32)],
        scratch_shapes=[pltpu.VMEM((8, LANES), F32)],
        compiler_params=_cparams(1),
        name="route",
    )(logits)


def _gather_kernel(pos1_ref, pos2_ref, h_hbm, o_ref, src_ref, buf_ref, sems, *,
                   n_tokens, nb):
    tg = o_ref.shape[0]
    i = pl.program_id(0)
    n_tiles = pl.num_programs(0) - 1

    @pl.when(i == 0)
    def _():
        def clear(p, carry):
            src_ref[p] = 0
            return carry
        lax.fori_loop(0, src_ref.shape[0], clear, 0, unroll=8)

        def place(t, carry):
            src_ref[pos1_ref[t]] = t
            src_ref[pos2_ref[t]] = t
            return carry
        lax.fori_loop(0, n_tokens, place, 0, unroll=4)

    def slab_copy(tile, slot, r):
        src = pl.multiple_of(src_ref[tile * tg + r] * nb, nb)
        dst = pl.multiple_of(r * nb, nb)
        return pltpu.make_async_copy(
            h_hbm.at[pl.ds(src, nb), :], buf_ref.at[slot, pl.ds(dst, nb), :], sems.at[slot])

    @pl.when(i < n_tiles)
    def _():
        def issue(r, carry):
            slab_copy(i, i % 2, r).start()
            return carry
        lax.fori_loop(0, tg, issue, 0, unroll=4)

    @pl.when(i > 0)
    def _():
        slot = (i - 1) % 2

        def drain(r, carry):
            slab_copy(i - 1, slot, r).wait()
            return carry
        lax.fori_loop(0, tg, drain, 0, unroll=4)
        half = nb * LANES
        for c in range(nb):
            lo, hi = _unpack_bf16_pairs(buf_ref[slot, pl.ds(c, tg, stride=nb), :])
            o_ref[:, c * LANES:(c + 1) * LANES] = lo
            o_ref[:, half + c * LANES:half + (c + 1) * LANES] = hi


def gather_rows(h_packed, pos1, pos2, n_rows, d, tg=256):
    nb = d // 2 // LANES
    n = h_packed.shape[0] // nb
    width = 2 * nb * LANES
    n_tiles = n_rows // tg
    grid_spec = pltpu.PrefetchScalarGridSpec(
        num_scalar_prefetch=2,
        grid=(n_tiles + 1,),
        in_specs=[pl.BlockSpec(memory_space=pl.ANY)],
        out_specs=pl.BlockSpec((tg, width), lambda i, p1, p2: (jnp.maximum(i - 1, 0), 0)),
        scratch_shapes=[pltpu.SMEM((n_rows,), I32), pltpu.VMEM((2, tg * nb, LANES), U32),
                        pltpu.SemaphoreType.DMA((2,))],
    )
    return pl.pallas_call(
        functools.partial(_gather_kernel, n_tokens=n, nb=nb),
        grid_spec=grid_spec,
        out_shape=jax.ShapeDtypeStruct((n_rows, width), BF16),
        compiler_params=_cparams(1),
        name="gather_rows",
    )(pos1, pos2, h_packed)


def _combine_kernel(pos1_ref, pos2_ref, o_hbm, gate_ref, x_ref, g_ref, out_ref, buf_ref, sems):
    tt = x_ref.shape[0]
    i = pl.program_id(0)
    n_tiles = pl.num_programs(0) - 1

    def row_copy(tile, slot, r, which, pos_ref):
        return pltpu.make_async_copy(
            o_hbm.at[pl.ds(pos_ref[tile * tt + r], 1), :],
            buf_ref.at[slot, which, pl.ds(r, 1), :], sems.at[slot])

    @pl.when(i < n_tiles)
    def _():
        def issue(r, carry):
            row_copy(i, i % 2, r, 0, pos1_ref).start()
            row_copy(i, i % 2, r, 1, pos2_ref).start()
            return carry
        lax.fori_loop(0, tt, issue, 0, unroll=4)

    @pl.when(i > 0)
    def _():
        slot = (i - 1) % 2

        def drain(r, carry):
            row_copy(i - 1, slot, r, 0, pos1_ref).wait()
            row_copy(i - 1, slot, r, 1, pos2_ref).wait()
            return carry
        lax.fori_loop(0, tt, drain, 0, unroll=4)
        gate = gate_ref[...]
        y = gate[:, 0:1] * buf_ref[slot, 0] + gate[:, 1:2] * buf_ref[slot, 1]
        out_ref[...] = x_ref[...] + (y * _rms(y)) * g_ref[...]


def combine(o, pos1, pos2, gate, x, g, tt=128):
    n, d = x.shape
    row = lambda i, p1, p2: (jnp.maximum(i - 1, 0), 0)
    grid_spec = pltpu.PrefetchScalarGridSpec(
        num_scalar_prefetch=2,
        grid=(n // tt + 1,),
        in_specs=[pl.BlockSpec(memory_space=pl.ANY),
                  pl.BlockSpec((tt, LANES), row),
                  pl.BlockSpec((tt, d), row),
                  pl.BlockSpec((1, d), lambda i, p1, p2: (0, 0))],
        out_specs=pl.BlockSpec((tt, d), row),
        scratch_shapes=[pltpu.VMEM((2, 2, tt, d), F32), pltpu.SemaphoreType.DMA((2,))],
    )
    return pl.pallas_call(
        _combine_kernel,
        grid_spec=grid_spec,
        out_shape=jax.ShapeDtypeStruct((n, d), F32),
        compiler_params=_cparams(1),
        name="combine",
    )(pos1, pos2, o, gate, x, g.reshape(1, d))


def _rope_tables(s):
    half = HEAD_DIM // 2
    inv = ROPE_BASE ** (-jnp.arange(half, dtype=F32) / half)
    ang = jnp.arange(s, dtype=F32)[:, None] * inv[None, :]
    cos, sin = jnp.cos(ang), jnp.sin(ang)
    return jnp.concatenate([cos, cos], axis=-1), jnp.concatenate([-sin, sin], axis=-1)


def _moe_schedule(ids, counts, tm, n_tiles, nc):
    ne = N_EXPERTS
    counts = counts[0, :ne]
    tiles = (counts + tm - 1) // tm
    nonempty = tiles > 0
    tile_end = jnp.cumsum(tiles)
    tile_start = tile_end - tiles
    pos1 = tile_start[ids[:, 0]] * tm + ids[:, 2]
    pos2 = tile_start[ids[:, 1]] * tm + ids[:, 3]

    steps = jnp.where(nonempty, jnp.maximum(tiles, nc), 0)
    step_end = jnp.cumsum(steps)
    step_start = step_end - steps
    total, n_used = step_end[-1], tile_end[-1]
    ar = jnp.arange(ne, dtype=I32)
    first_e = jnp.min(jnp.where(nonempty, ar, ne))
    last_e = jnp.max(jnp.where(nonempty, ar, -1))
    later = (ar[None, :] > ar[:, None]) & nonempty[None, :]
    nxt = jnp.min(jnp.where(later, ar[None, :], ne), axis=1)
    wrap = nxt == ne
    nxt = jnp.where(wrap, first_e, nxt)
    rank = jnp.cumsum(nonempty.astype(I32)) - 1

    s = jnp.arange(n_tiles + ne * (nc - 1), dtype=I32)
    in_blocks = s < total
    e = jnp.minimum(jnp.sum((s[:, None] >= step_end[None, :]).astype(I32), axis=1), ne - 1)
    e = jnp.where(in_blocks, e, last_e)
    p = s - step_start[e]
    is_comp = in_blocks & (p < tiles[e])
    z = s - total
    is_zero = (~in_blocks) & (z < n_tiles - n_used)
    group_last = tile_start[e] + tiles[e] - 1
    tile = jnp.where(is_comp, tile_start[e] + p,
                     jnp.where(is_zero, n_used + z, jnp.where(in_blocks, group_last, n_tiles - 1)))
    arow = jnp.where(is_comp, tile, jnp.where(in_blocks, group_last, n_used - 1))
    kind = jnp.where(is_comp, KIND_COMPUTE, jnp.where(is_zero, KIND_ZERO, KIND_NONE))
    pf_chunk = jnp.where(in_blocks, jnp.minimum(p, nc - 1), nc - 1)
    pf_do = in_blocks & (p < nc)
    sched = (tile, arow, kind, rank[e], nxt[e], wrap[e], pf_chunk, pf_do,
             jnp.sum(nonempty.astype(I32)).reshape(1))
    n_steps = (total + n_tiles - n_used).astype(I32)
    return pos1.astype(I32), pos2.astype(I32), tuple(v.astype(I32) for v in sched), n_steps


def kernel(x, even_norm_mix_pre, even_w_in, even_b_forget, even_w_out, even_norm_mix_post,
           even_norm_ffn_pre, even_w_gate, even_w_up, even_w_down, even_norm_ffn_post,
           odd_norm_mix_pre, odd_w_pool, odd_pool_scale, odd_norm_mix_post, odd_norm_ffn_pre,
           odd_w_router, odd_we_gate, odd_we_up, odd_we_down, odd_norm_ffn_post):
    batch, seq, d = x.shape
    n = batch * seq
    x0 = x.reshape(n, d)
    ret_w = N_RET_HEADS * HEAD_DIM
    fox_w = N_FOX_HEADS * HEAD_DIM
    main_cols = 4 * ret_w + 3 * fox_w

    h0 = rms_cast(x0, even_norm_mix_pre[0])
    w_in_t = jnp.swapaxes(even_w_in, 1, 2)
    proj = gmm(h0, [w_in_t], tm=1024, tn=1024, n_out=main_cols, out_dtype=BF16, nc=4,
               transposed=True)
    b_forget = jnp.pad(even_b_forget[0], (0, LANES - N_FOX_HEADS)).reshape(1, LANES)
    f_logit = gmm(h0, [w_in_t], tm=512, tn=LANES, n_out=LANES, out_dtype=F32, nc=1,
                  transposed=True, w_col=lambda j: main_cols // LANES + j,
                  n_valid=N_FOX_HEADS)
    c_cols = forget_cumsum(f_logit, b_forget, batch)
    c_rows = c_cols[:, :N_FOX_HEADS].reshape(batch, seq, N_FOX_HEADS)
    c_rows = c_rows.transpose(0, 2, 1).reshape(batch, N_FOX_HEADS, 1, seq)
    cos, sin = _rope_tables(seq)
    log_gamma = jnp.log1p(-(2.0 ** (-5.0 - jnp.arange(N_RET_HEADS, dtype=F32))))
    ret = retention(proj, cos, sin, log_gamma, batch)
    fox = fox_attention(proj, c_cols, c_rows, batch)
    m = gmm([ret, fox], [even_w_out], tm=1024, tn=1024, n_out=d, out_dtype=F32, nc=8)
    x1, h1 = resid_norm(x0, m, even_norm_mix_post[0], even_norm_ffn_pre[0], "bf16")

    d_ff = even_w_gate.shape[-1]
    act = gmm(h1, [even_w_gate, even_w_up], tm=1024, tn=512, n_out=d_ff, out_dtype=BF16, nc=4)
    f = gmm(act, [even_w_down], tm=512, tn=512, n_out=d, out_dtype=F32, nc=16)
    x2, pooled = resid_norm(x1, f, even_norm_ffn_post[0], odd_norm_mix_pre[0], "pooled",
                            seq=seq)
    group = d // len(POOL_WINDOWS)
    tn_pool = 512
    per = group // tn_pool
    m2 = gmm(pooled, [odd_w_pool[0]], tm=2048, tn=tn_pool, n_out=d, out_dtype=F32, nc=2,
             k=group, scale=odd_pool_scale[0].reshape(1, d),
             a_col=lambda j: j // per, w_group=lambda j: j // per, w_col=lambda j: j % per)
    w_router = jnp.pad(odd_w_router[0], ((0, 0), (0, LANES - N_EXPERTS)))
    x3, h3, logits = resid_norm(x2, m2, odd_norm_mix_post[0], odd_norm_ffn_pre[0], "packed",
                                w_router=w_router)

    ids, gate, counts = route(logits)
    tm = MOE_TILE
    n_rows = 2 * n + N_EXPERTS * tm
    pos1, pos2, sched, n_steps = _moe_schedule(ids, counts, tm, n_rows // tm, MOE_CHUNKS)
    xs = gather_rows(h3, pos1, pos2, n_rows, d)
    d_fe = odd_we_gate.shape[-1]
    act2 = gmm(xs, [odd_we_gate[0], odd_we_up[0]], tm=tm, tn=1024, n_out=d_fe,
               out_dtype=BF16, nc=MOE_CHUNKS, sched=sched, merge_cast=False, n_steps=n_steps)
    o = gmm(act2, [odd_we_down[0]], tm=tm, tn=1024, n_out=d, out_dtype=F32,
            nc=MOE_CHUNKS, sched=sched, merge_cast=False, n_steps=n_steps)
    out = combine(o, pos1, pos2, gate, x3, odd_norm_ffn_post[0])
    return out.reshape(batch, seq, d)
```

```python
import functools

import jax
import jax.numpy as jnp
from jax import lax
from jax.experimental import pallas as pl
from jax.experimental.pallas import tpu as pltpu

F32 = jnp.float32
BF16 = jnp.bfloat16
U32 = jnp.uint32
I32 = jnp.int32

HEAD_DIM = 128
N_RET_HEADS = 16
N_FOX_HEADS = 16
RET_CHUNK = 128
ROPE_BASE = 10000.0
N_EXPERTS = 8
POOL_WINDOWS = (2, 4, 8, 16)
EPS = 1e-6
LANES = 128
NEG_BIG = -1e30

VMEM_LIMIT = 56 * 1024 * 1024
MOE_TILE = 256
MOE_CHUNKS = 8


def _cparams(n_axes):
    return pltpu.CompilerParams(
        dimension_semantics=("arbitrary",) * n_axes, vmem_limit_bytes=VMEM_LIMIT)


def _rms(v):
    return lax.rsqrt(jnp.mean(v * v, axis=-1, keepdims=True) + EPS)


def _sigmoid(v):
    return 1.0 / (1.0 + jnp.exp(-v))


def _rms_cast_kernel(x_ref, g_ref, o_ref):
    x = x_ref[...]
    o_ref[...] = ((x * _rms(x)) * g_ref[...]).astype(o_ref.dtype)


def rms_cast(x, g, tm=256):
    n, d = x.shape
    return pl.pallas_call(
        _rms_cast_kernel,
        grid=(n // tm,),
        in_specs=[pl.BlockSpec((tm, d), lambda i: (i, 0)),
                  pl.BlockSpec((1, d), lambda i: (0, 0))],
        out_specs=pl.BlockSpec((tm, d), lambda i: (i, 0)),
        out_shape=jax.ShapeDtypeStruct((n, d), BF16),
        compiler_params=_cparams(1),
        name="rms_cast",
    )(x, g.reshape(1, d))


def _pack_bf16_pairs(h):
    half = h.shape[1] // 2
    bits = lax.bitcast_convert_type(h.astype(BF16).astype(F32), U32)
    return (bits[:, :half] >> 16) | (bits[:, half:] & jnp.uint32(0xFFFF0000))


def _unpack_bf16_pairs(p):
    lo = lax.bitcast_convert_type(p << 16, F32).astype(BF16)
    hi = lax.bitcast_convert_type(p & jnp.uint32(0xFFFF0000), F32).astype(BF16)
    return lo, hi


def _pool_tile(h, o_ref, halo_ref, tiles_per_seq):
    tm, d = h.shape
    group = d // len(POOL_WINDOWS)
    seq_tile = pl.program_id(0) % tiles_per_seq

    @pl.when(seq_tile == 0)
    def _():
        halo_ref[...] = jnp.zeros_like(halo_ref)

    row = lax.broadcasted_iota(I32, (tm, group), 0)
    row8 = lax.broadcasted_iota(I32, (8, group), 0)
    count_cap = seq_tile * tm + row + 1
    for gi, w in enumerate(POOL_WINDOWS):
        cols = slice(gi * group, (gi + 1) * group)
        x = h[:, cols]
        acc = x
        by, level = 1, 0
        while by < w:
            prev = halo_ref[level, :, cols]
            rolled = pltpu.roll(acc, by, 0)
            prev_rolled = pltpu.roll(prev, by, 0) if by < 8 else prev
            top = jnp.where(row8 < by, prev_rolled, rolled[0:8])
            halo_ref[level, :, cols] = acc[tm - 8:tm]
            acc = acc + jnp.concatenate([top, rolled[8:]], axis=0)
            by *= 2
            level += 1
        count = jnp.minimum(count_cap, w).astype(F32)
        o_ref[:, cols] = (acc / count - x).astype(o_ref.dtype)


def _resid_norm_kernel(*refs, h_mode, router, tiles_per_seq):
    x_ref, m_ref, gp_ref = refs[:3]
    pos = 3
    gn_ref = wr_ref = None
    if h_mode is not None:
        gn_ref = refs[pos]; pos += 1
    if router:
        wr_ref = refs[pos]; pos += 1
    xo_ref = refs[pos]; pos += 1
    m = m_ref[...]
    xn = x_ref[...] + (m * _rms(m)) * gp_ref[...]
    xo_ref[...] = xn
    if h_mode is None:
        return
    h = (xn * _rms(xn)) * gn_ref[...]
    h_ref = refs[pos]; pos += 1
    if h_mode == "packed":
        packed = _pack_bf16_pairs(h)
        nb = packed.shape[1] // LANES
        for c in range(nb):
            h_ref[pl.ds(c, packed.shape[0], stride=nb), :] = packed[:, c * LANES:(c + 1) * LANES]
    elif h_mode == "pooled":
        _pool_tile(h, h_ref, refs[-1], tiles_per_seq)
    else:
        h_ref[...] = h.astype(h_ref.dtype)
    if router:
        refs[pos][...] = jnp.dot(h, wr_ref[...], preferred_element_type=F32,
                                 precision=lax.Precision.HIGHEST)


def resid_norm(x, m, g_post, g_next=None, h_mode=None, w_router=None, seq=None, tm=256):
    n, d = x.shape
    row = pl.BlockSpec((tm, d), lambda i: (i, 0))
    vec = pl.BlockSpec((1, d), lambda i: (0, 0))
    args = [x, m, g_post.reshape(1, d)]
    in_specs = [row, row, vec]
    out_shape = [jax.ShapeDtypeStruct((n, d), F32)]
    out_specs = [row]
    if h_mode is not None:
        args.append(g_next.reshape(1, d)); in_specs.append(vec)
        if h_mode == "packed":
            nb = d // 2 // LANES
            out_shape.append(jax.ShapeDtypeStruct((n * nb, LANES), U32))
            out_specs.append(pl.BlockSpec((tm * nb, LANES), lambda i: (i, 0)))
        else:
            out_shape.append(jax.ShapeDtypeStruct((n, d), BF16))
            out_specs.append(row)
    scratch = []
    if h_mode == "pooled":
        n_levels = len(POOL_WINDOWS)
        scratch.append(pltpu.VMEM((n_levels, 8, d), F32))
    router = w_router is not None
    if router:
        args.append(w_router)
        in_specs.append(pl.BlockSpec((d, LANES), lambda i: (0, 0)))
        out_shape.append(jax.ShapeDtypeStruct((n, LANES), F32))
        out_specs.append(pl.BlockSpec((tm, LANES), lambda i: (i, 0)))
    return pl.pallas_call(
        functools.partial(_resid_norm_kernel, h_mode=h_mode, router=router,
                          tiles_per_seq=None if seq is None else seq // tm),
        grid=(n // tm,),
        in_specs=in_specs, out_specs=out_specs, out_shape=out_shape,
        scratch_shapes=scratch,
        compiler_params=_cparams(1),
        name="resid_norm",
    )(*args)


N_SCHED = 9
KIND_NONE, KIND_COMPUTE, KIND_ZERO, KIND_HALF = 0, 1, 2, 3


def _gmm_kernel(tile_ref, arow_ref, kind_ref, blk_ref, pfg_ref, pfw_ref, pfc_ref, pfd_ref,
                nblk_ref, *refs, n_a, n_w, n_col, has_scale, transposed, n_valid, merge_cast,
                half_tiles):
    del tile_ref, arow_ref, pfg_ref
    a_refs = refs[:n_a]
    c_refs = refs[n_a:n_a + n_w]
    pos = n_a + n_w
    s_ref = None
    if has_scale:
        s_ref = refs[pos]; pos += 1
    o_ref = refs[pos]; pos += 1
    slots = (refs[pos:pos + n_w], refs[pos + n_w:pos + 2 * n_w])
    jj = pl.program_id(0)
    s = pl.program_id(1)
    parity = (jj * nblk_ref[0] + blk_ref[s] + nblk_ref[0]) % 2
    rows_per_chunk = c_refs[0].shape[1]
    tcol = jj - 1 + pfw_ref[s]
    chunk_due = (pfd_ref[s] == 1) & (tcol >= 0) & (tcol < n_col)
    kind = kind_ref[s]
    computing = ((kind == KIND_COMPUTE) | (kind == KIND_HALF)) & (jj >= 1)
    tm = o_ref.shape[0]

    def cast_chunk(nxt):
        r = pl.multiple_of(pfc_ref[s] * rows_per_chunk, rows_per_chunk)
        for c_ref, wbf in zip(c_refs, nxt):
            wbf[pl.ds(r, rows_per_chunk), :] = c_ref[0].astype(BF16)

    def matmul(wbf, rows):
        if transposed:
            return lax.dot_general(a_refs[0][0:rows, :], wbf[...], (((1,), (1,)), ((), ())),
                                   preferred_element_type=F32)
        acc = None
        off = 0
        for a_ref in a_refs:
            ka = a_ref.shape[1]
            part = jnp.dot(a_ref[0:rows, :], wbf[off:off + ka, :], preferred_element_type=F32)
            acc = part if acc is None else acc + part
            off += ka
        return acc

    def compute(cur, nxt, rows):
        if merge_cast:
            cast_chunk(nxt)
        acc = matmul(cur[0], rows)
        if n_w == 2:
            acc = (acc * _sigmoid(acc)) * matmul(cur[1], rows)
        if has_scale:
            acc = acc * s_ref[...]
        if n_valid is not None:
            col = lax.broadcasted_iota(I32, acc.shape, 1) + (jj - 1) * acc.shape[1]
            acc = jnp.where(col < n_valid, acc, 0.0)
        o_ref[0:rows, :] = acc.astype(o_ref.dtype)
        if rows < tm:
            o_ref[rows:tm, :] = jnp.zeros((tm - rows, o_ref.shape[1]), o_ref.dtype)

    heights = ((KIND_COMPUTE, tm), (KIND_HALF, tm // 2)) if half_tiles else ((KIND_COMPUTE, tm),)
    for p in (0, 1):
        cur, nxt = slots[p], slots[1 - p]
        for which, rows in heights:
            pl.when((kind == which) & (jj >= 1) & (parity == p))(
                functools.partial(compute, cur, nxt, rows))
        cast_here = chunk_due & jnp.logical_not(computing) if merge_cast else chunk_due
        pl.when(cast_here & (parity == p))(functools.partial(cast_chunk, nxt))

    @pl.when((kind_ref[s] == KIND_ZERO) & (jj >= 1))
    def _():
        o_ref[...] = jnp.zeros_like(o_ref)


def _dense_schedule(n_tiles, nc):
    t = jnp.arange(n_tiles, dtype=I32)
    zeros = jnp.zeros((n_tiles,), I32)
    ones = jnp.ones((n_tiles,), I32)
    return (t, t, ones, zeros, zeros, ones, jnp.minimum(t, nc - 1), (t < nc).astype(I32),
            jnp.ones((1,), I32))


def gmm(a, ws, *, tm, tn, n_out, out_dtype, nc, sched=None, k=None, scale=None,
        a_col=None, w_group=None, w_col=None, transposed=False, n_valid=None,
        merge_cast=True, n_steps=None, half_tiles=False):
    a_parts = list(a) if isinstance(a, (list, tuple)) else [a]
    rows = a_parts[0].shape[0]
    if k is None:
        k = sum(p.shape[1] for p in a_parts)
    if sched is None:
        sched = _dense_schedule(rows // tm, nc)
    if n_steps is None:
        n_steps = sched[0].shape[0]
    n_col = pl.cdiv(n_out, tn)
    a_col = a_col or (lambda j: 0)
    w_col = w_col or (lambda j: j)

    def cur_col(jj):
        return jnp.maximum(jj - 1, 0)

    def a_map(jj, s, tile, arow, *_):
        return (jnp.where(jj == 0, arow[0], arow[s]), a_col(cur_col(jj)))

    def o_map(jj, s, tile, *_):
        return (jnp.where(jj == 0, tile[0], tile[s]), cur_col(jj))

    def c_map(jj, s, tile, arow, kind, blk, pfg, pfw, pfc, *_):
        col = jnp.clip(jj - 1 + pfw[s], 0, n_col - 1)
        group = pfg[s] if w_group is None else w_group(col)
        if transposed:
            return (group, w_col(col) * nc + pfc[s], 0)
        return (group, pfc[s], w_col(col))

    in_specs = [pl.BlockSpec((tm, k if len(a_parts) == 1 else p.shape[1]), a_map)
                for p in a_parts]
    if transposed:
        assert len(a_parts) == 1 and tn % nc == 0
        chunk, wbf_shape = (1, tn // nc, k), (tn, k)
    else:
        assert k % nc == 0
        chunk, wbf_shape = (1, k // nc, tn), (k, tn)
    in_specs += [pl.BlockSpec(chunk, c_map) for _ in ws]
    args = [*a_parts, *ws]
    if scale is not None:
        in_specs.append(pl.BlockSpec((1, tn), lambda jj, s, *_: (0, cur_col(jj))))
        args.append(scale)
    grid_spec = pltpu.PrefetchScalarGridSpec(
        num_scalar_prefetch=N_SCHED,
        grid=(n_col + 1, n_steps),
        in_specs=in_specs,
        out_specs=pl.BlockSpec((tm, tn), o_map),
        scratch_shapes=[pltpu.VMEM(wbf_shape, BF16) for _ in range(2 * len(ws))],
    )
    return pl.pallas_call(
        functools.partial(_gmm_kernel, n_a=len(a_parts), n_w=len(ws), n_col=n_col,
                          has_scale=scale is not None, transposed=transposed,
                          n_valid=n_valid, merge_cast=merge_cast, half_tiles=half_tiles),
        grid_spec=grid_spec,
        out_shape=jax.ShapeDtypeStruct((rows, n_out), out_dtype),
        compiler_params=_cparams(2),
        name="gmm",
    )(*sched, *args)


def _forget_cumsum_kernel(f_ref, b_ref, o_ref):
    x = f_ref[...] + b_ref[...]
    ls = jnp.minimum(x, 0.0) - jnp.log1p(jnp.exp(-jnp.abs(x)))
    s = ls.shape[0]
    row = lax.broadcasted_iota(I32, ls.shape, 0)
    sh = 1
    while sh < s:
        ls = ls + jnp.where(row >= sh, pltpu.roll(ls, sh, 0), 0.0)
        sh *= 2
    o_ref[...] = ls


def forget_cumsum(f, b, batch):
    n, w = f.shape
    s = n // batch
    return pl.pallas_call(
        _forget_cumsum_kernel,
        grid=(batch,),
        in_specs=[pl.BlockSpec((s, w), lambda i: (i, 0)),
                  pl.BlockSpec((1, w), lambda i: (0, 0))],
        out_specs=pl.BlockSpec((s, w), lambda i: (i, 0)),
        out_shape=jax.ShapeDtypeStruct((n, w), F32),
        compiler_params=_cparams(1),
        name="forget_cumsum",
    )(f, b)


def _retention_kernel(lg_ref, q_ref, k_ref, v_ref, g_ref, cos_ref, sin_ref, o_ref):
    c = RET_CHUNK
    d = HEAD_DIM
    s = q_ref.shape[0]
    lg = lg_ref[pl.program_id(1)]
    ii = lax.broadcasted_iota(I32, (c, c), 0)
    jj = lax.broadcasted_iota(I32, (c, c), 1)
    diff = (ii - jj).astype(F32)
    decay_in = jnp.where(diff >= 0, jnp.exp(lg * jnp.maximum(diff, 0.0)), 0.0)
    jc = lax.broadcasted_iota(I32, (c, 1), 0).astype(F32)
    zeta = jnp.exp(lg * (c - 1 - jc))
    q_decay = jnp.exp(lg * (jc + 1.0))
    gamma_c = jnp.exp(jnp.full((1, 1), lg, F32) * c)
    scale = d ** -0.5
    nt = (((1,), (1,)), ((), ()))
    tn = (((0,), (0,)), ((), ()))
    state = jnp.zeros((d, d), F32)
    for n in range(s // c):
        sl = pl.ds(n * c, c)
        cos = cos_ref[sl, :]
        sin = sin_ref[sl, :]
        q = q_ref[sl, :].astype(F32)
        k = k_ref[sl, :].astype(F32)
        v = v_ref[sl, :]
        qr = q * cos + pltpu.roll(q, d // 2, 1) * sin
        kr = (k * cos + pltpu.roll(k, d // 2, 1) * sin) * scale
        scores = lax.dot_general(qr.astype(BF16), kr.astype(BF16), nt,
                                 preferred_element_type=F32) * decay_in
        inner = jnp.dot(scores.astype(BF16), v, preferred_element_type=F32)
        cross = jnp.dot((qr * q_decay).astype(BF16), state.astype(BF16),
                        preferred_element_type=F32)
        kv = lax.dot_general((kr * zeta).astype(BF16), v, tn, preferred_element_type=F32)
        state = gamma_c * state + kv
        o = inner + cross
        xc = o - jnp.mean(o, axis=-1, keepdims=True)
        y = xc * lax.rsqrt(jnp.mean(xc * xc, axis=-1, keepdims=True) + EPS)
        g = g_ref[sl, :].astype(F32)
        o_ref[sl, :] = ((g * _sigmoid(g)) * y).astype(o_ref.dtype)


def retention(proj, cos, sin, log_gamma, batch):
    n = proj.shape[0]
    s = n // batch
    d = HEAD_DIM
    hh = N_RET_HEADS

    def col(off):
        return pl.BlockSpec((s, d), lambda b, h, lg: (b, off + h))

    tab = pl.BlockSpec((s, d), lambda b, h, lg: (0, 0))
    grid_spec = pltpu.PrefetchScalarGridSpec(
        num_scalar_prefetch=1,
        grid=(batch, hh),
        in_specs=[col(0), col(hh), col(2 * hh), col(3 * hh), tab, tab],
        out_specs=pl.BlockSpec((s, d), lambda b, h, lg: (b, h)),
    )
    return pl.pallas_call(
        _retention_kernel,
        grid_spec=grid_spec,
        out_shape=jax.ShapeDtypeStruct((n, hh * d), BF16),
        compiler_params=_cparams(2),
        name="retention",
    )(log_gamma, proj, proj, proj, proj, cos, sin)


def _fox_kernel(q_ref, k_ref, v_ref, ccol_ref, crow_ref, o_ref, *, tq):
    s, d = q_ref.shape
    h = pl.program_id(1)
    scale = d ** -0.5
    nt = (((1,), (1,)), ((), ()))
    lane = lax.broadcasted_iota(I32, (tq, LANES), 1)
    rr = lax.broadcasted_iota(I32, (tq, tq), 0)
    cc = lax.broadcasted_iota(I32, (tq, tq), 1)

    for qi in range(s // tq):
        qs = pl.ds(qi * tq, tq)
        q = q_ref[qs, :]
        cq = jnp.sum(jnp.where(lane == h, ccol_ref[qs, :], 0.0), axis=-1, keepdims=True)

        def scores(kb):
            ks = pl.ds(pl.multiple_of(kb * tq, tq), tq)
            sc = lax.dot_general(q, k_ref[ks, :], nt, preferred_element_type=F32) * scale
            return sc + cq - crow_ref[:, ks], v_ref[ks, :]

        def update(carry, sc, v):
            m, l, acc = carry
            m_new = jnp.maximum(m, jnp.max(sc, axis=-1, keepdims=True))
            alpha = jnp.exp(m - m_new)
            p = jnp.exp(sc - m_new)
            l = alpha * l + jnp.sum(p, axis=-1, keepdims=True)
            acc = alpha * acc + jnp.dot(p.astype(BF16), v, preferred_element_type=F32)
            return m_new, l, acc

        def body(kb, carry):
            sc, v = scores(kb)
            return update(carry, sc, v)

        carry = (jnp.full((tq, 1), NEG_BIG, F32), jnp.zeros((tq, 1), F32),
                 jnp.zeros((tq, d), F32))
        carry = lax.fori_loop(0, qi, body, carry, unroll=True)
        sc, v = scores(qi)
        _, l, acc = update(carry, jnp.where(rr >= cc, sc, NEG_BIG), v)
        o_ref[qs, :] = (acc / l).astype(o_ref.dtype)


def fox_attention(proj, c_cols, c_rows, batch, tq=256):
    n = proj.shape[0]
    s = n // batch
    d = HEAD_DIM
    base = 4 * N_RET_HEADS
    hh = N_FOX_HEADS

    def col(off):
        return pl.BlockSpec((s, d), lambda b, h: (b, off + h))

    return pl.pallas_call(
        functools.partial(_fox_kernel, tq=tq),
        grid=(batch, hh),
        in_specs=[col(base), col(base + hh), col(base + 2 * hh),
                  pl.BlockSpec((s, LANES), lambda b, h: (b, 0)),
                  pl.BlockSpec((None, None, 1, s), lambda b, h: (b, h, 0, 0))],
        out_specs=pl.BlockSpec((s, d), lambda b, h: (b, h)),
        out_shape=jax.ShapeDtypeStruct((n, hh * d), BF16),
        compiler_params=_cparams(2),
        name="fox_attention",
    )(proj, proj, proj, c_cols, c_rows)


def _route_kernel(lg_ref, ids_ref, gate_ref, cnt_ref, carry_ref):
    tt = lg_ref.shape[0]

    @pl.when(pl.program_id(0) == 0)
    def _():
        carry_ref[...] = jnp.zeros_like(carry_ref)

    lane = lax.broadcasted_iota(I32, (tt, LANES), 1)
    lane_f = lane.astype(F32)
    x = jnp.where(lane < N_EXPERTS, lg_ref[...], -jnp.inf)
    v1 = jnp.max(x, axis=-1, keepdims=True)
    i1 = jnp.min(jnp.where(x == v1, lane_f, float(LANES)), axis=-1, keepdims=True)
    x2 = jnp.where(lane_f == i1, -jnp.inf, x)
    v2 = jnp.max(x2, axis=-1, keepdims=True)
    i2 = jnp.min(jnp.where(x2 == v2, lane_f, float(LANES)), axis=-1, keepdims=True)
    e = jnp.exp(v2 - v1)
    w1 = 1.0 / (1.0 + e)
    w2 = e / (1.0 + e)
    sel1 = lane_f == i1
    sel2 = lane_f == i2
    onehot = jnp.where(sel1, 1.0, jnp.where(sel2, 1.0, 0.0))
    rr = lax.broadcasted_iota(I32, (tt, tt), 0)
    cc = lax.broadcasted_iota(I32, (tt, tt), 1)
    tri = jnp.where(rr > cc, 1.0, 0.0).astype(BF16)
    before = jnp.dot(tri, onehot.astype(BF16), preferred_element_type=F32) + carry_ref[0:1, :]
    r1 = jnp.sum(jnp.where(sel1, before, 0.0), axis=-1, keepdims=True)
    r2 = jnp.sum(jnp.where(sel2, before, 0.0), axis=-1, keepdims=True)
    ids = jnp.where(lane == 0, i1, jnp.where(lane == 1, i2,
                    jnp.where(lane == 2, r1, jnp.where(lane == 3, r2, 0.0))))
    ids_ref[...] = ids.astype(I32)
    gate_ref[...] = jnp.where(lane == 0, w1, jnp.where(lane == 1, w2, 0.0))
    total = carry_ref[0:1, :] + jnp.sum(onehot, axis=0, keepdims=True)
    carry_ref[...] = jnp.broadcast_to(total, carry_ref.shape)
    cnt_ref[...] = jnp.broadcast_to(total, cnt_ref.shape).astype(I32)


def route(logits, tt=256):
    n = logits.shape[0]
    row = pl.BlockSpec((tt, LANES), lambda i: (i, 0))
    return pl.pallas_call(
        _route_kernel,
        grid=(n // tt,),
        in_specs=[row],
        out_specs=[row, row, pl.BlockSpec((8, LANES), lambda i: (0, 0))],
        out_shape=[jax.ShapeDtypeStruct((n, LANES), I32),
                   jax.ShapeDtypeStruct((n, LANES), F32),
                   jax.ShapeDtypeStruct((8, LANES), I32)],
        scratch_shapes=[pltpu.VMEM((8, LANES), F32)],
        compiler_params=_cparams(1),
        name="route",
    )(logits)


def _gather_kernel(pos1_ref, pos2_ref, h_hbm, o_ref, src_ref, buf_ref, sems, *,
                   n_tokens, nb):
    tg = o_ref.shape[0]
    i = pl.program_id(0)
    n_tiles = pl.num_programs(0) - 1

    @pl.when(i == 0)
    def _():
        def clear(p, carry):
            src_ref[p] = 0
            return carry
        lax.fori_loop(0, src_ref.shape[0], clear, 0, unroll=8)

        def place(t, carry):
            src_ref[pos1_ref[t]] = t
            src_ref[pos2_ref[t]] = t
            return carry
        lax.fori_loop(0, n_tokens, place, 0, unroll=4)

    def slab_copy(tile, slot, r):
        src = pl.multiple_of(src_ref[tile * tg + r] * nb, nb)
        dst = pl.multiple_of(r * nb, nb)
        return pltpu.make_async_copy(
            h_hbm.at[pl.ds(src, nb), :], buf_ref.at[slot, pl.ds(dst, nb), :], sems.at[slot])

    @pl.when(i < n_tiles)
    def _():
        def issue(r, carry):
            slab_copy(i, i % 2, r).start()
            return carry
        lax.fori_loop(0, tg, issue, 0, unroll=4)

    @pl.when(i > 0)
    def _():
        slot = (i - 1) % 2

        def drain(r, carry):
            slab_copy(i - 1, slot, r).wait()
            return carry
        lax.fori_loop(0, tg, drain, 0, unroll=4)
        half = nb * LANES
        for c in range(nb):
            lo, hi = _unpack_bf16_pairs(buf_ref[slot, pl.ds(c, tg, stride=nb), :])
            o_ref[:, c * LANES:(c + 1) * LANES] = lo
            o_ref[:, half + c * LANES:half + (c + 1) * LANES] = hi


def gather_rows(h_packed, pos1, pos2, n_rows, d, tg=256):
    nb = d // 2 // LANES
    n = h_packed.shape[0] // nb
    width = 2 * nb * LANES
    n_tiles = n_rows // tg
    grid_spec = pltpu.PrefetchScalarGridSpec(
        num_scalar_prefetch=2,
        grid=(n_tiles + 1,),
        in_specs=[pl.BlockSpec(memory_space=pl.ANY)],
        out_specs=pl.BlockSpec((tg, width), lambda i, p1, p2: (jnp.maximum(i - 1, 0), 0)),
        scratch_shapes=[pltpu.SMEM((n_rows,), I32), pltpu.VMEM((2, tg * nb, LANES), U32),
                        pltpu.SemaphoreType.DMA((2,))],
    )
    return pl.pallas_call(
        functools.partial(_gather_kernel, n_tokens=n, nb=nb),
        grid_spec=grid_spec,
        out_shape=jax.ShapeDtypeStruct((n_rows, width), BF16),
        compiler_params=_cparams(1),
        name="gather_rows",
    )(pos1, pos2, h_packed)


def _combine_kernel(pos1_ref, pos2_ref, o_hbm, gate_ref, x_ref, g_ref, out_ref, buf_ref, sems):
    tt = x_ref.shape[0]
    i = pl.program_id(0)
    n_tiles = pl.num_programs(0) - 1

    def row_copy(tile, slot, r, which, pos_ref):
        return pltpu.make_async_copy(
            o_hbm.at[pl.ds(pos_ref[tile * tt + r], 1), :],
            buf_ref.at[slot, which, pl.ds(r, 1), :], sems.at[slot])

    @pl.when(i < n_tiles)
    def _():
        def issue(r, carry):
            row_copy(i, i % 2, r, 0, pos1_ref).start()
            row_copy(i, i % 2, r, 1, pos2_ref).start()
            return carry
        lax.fori_loop(0, tt, issue, 0, unroll=4)

    @pl.when(i > 0)
    def _():
        slot = (i - 1) % 2

        def drain(r, carry):
            row_copy(i - 1, slot, r, 0, pos1_ref).wait()
            row_copy(i - 1, slot, r, 1, pos2_ref).wait()
            return carry
        lax.fori_loop(0, tt, drain, 0, unroll=4)
        gate = gate_ref[...]
        y = gate[:, 0:1] * buf_ref[slot, 0] + gate[:, 1:2] * buf_ref[slot, 1]
        out_ref[...] = x_ref[...] + (y * _rms(y)) * g_ref[...]


def combine(o, pos1, pos2, gate, x, g, tt=128):
    n, d = x.shape
    row = lambda i, p1, p2: (jnp.maximum(i - 1, 0), 0)
    grid_spec = pltpu.PrefetchScalarGridSpec(
        num_scalar_prefetch=2,
        grid=(n // tt + 1,),
        in_specs=[pl.BlockSpec(memory_space=pl.ANY),
                  pl.BlockSpec((tt, LANES), row),
                  pl.BlockSpec((tt, d), row),
                  pl.BlockSpec((1, d), lambda i, p1, p2: (0, 0))],
        out_specs=pl.BlockSpec((tt, d), row),
        scratch_shapes=[pltpu.VMEM((2, 2, tt, d), F32), pltpu.SemaphoreType.DMA((2,))],
    )
    return pl.pallas_call(
        _combine_kernel,
        grid_spec=grid_spec,
        out_shape=jax.ShapeDtypeStruct((n, d), F32),
        compiler_params=_cparams(1),
        name="combine",
    )(pos1, pos2, o, gate, x, g.reshape(1, d))


def _rope_tables(s):
    half = HEAD_DIM // 2
    inv = ROPE_BASE ** (-jnp.arange(half, dtype=F32) / half)
    ang = jnp.arange(s, dtype=F32)[:, None] * inv[None, :]
    cos, sin = jnp.cos(ang), jnp.sin(ang)
    return jnp.concatenate([cos, cos], axis=-1), jnp.concatenate([-sin, sin], axis=-1)


def _moe_schedule(ids, counts, tm, n_tiles, nc):
    ne = N_EXPERTS
    counts = counts[0, :ne]
    tiles = (counts + tm - 1) // tm
    nonempty = tiles > 0
    tile_end = jnp.cumsum(tiles)
    tile_start = tile_end - tiles
    pos1 = tile_start[ids[:, 0]] * tm + ids[:, 2]
    pos2 = tile_start[ids[:, 1]] * tm + ids[:, 3]

    steps = jnp.where(nonempty, jnp.maximum(tiles, nc), 0)
    step_end = jnp.cumsum(steps)
    step_start = step_end - steps
    total, n_used = step_end[-1], tile_end[-1]
    ar = jnp.arange(ne, dtype=I32)
    first_e = jnp.min(jnp.where(nonempty, ar, ne))
    last_e = jnp.max(jnp.where(nonempty, ar, -1))
    later = (ar[None, :] > ar[:, None]) & nonempty[None, :]
    nxt = jnp.min(jnp.where(later, ar[None, :], ne), axis=1)
    wrap = nxt == ne
    nxt = jnp.where(wrap, first_e, nxt)
    rank = jnp.cumsum(nonempty.astype(I32)) - 1

    s = jnp.arange(n_tiles + ne * (nc - 1), dtype=I32)
    in_blocks = s < total
    e = jnp.minimum(jnp.sum((s[:, None] >= step_end[None, :]).astype(I32), axis=1), ne - 1)
    e = jnp.where(in_blocks, e, last_e)
    p = s - step_start[e]
    is_comp = in_blocks & (p < tiles[e])
    z = s - total
    is_zero = (~in_blocks) & (z < n_tiles - n_used)
    group_last = tile_start[e] + tiles[e] - 1
    tile = jnp.where(is_comp, tile_start[e] + p,
                     jnp.where(is_zero, n_used + z, jnp.where(in_blocks, group_last, n_tiles - 1)))
    arow = jnp.where(is_comp, tile, jnp.where(in_blocks, group_last, n_used - 1))
    tail_rows = counts[e] - (tiles[e] - 1) * tm
    is_half = is_comp & (p == tiles[e] - 1) & (tail_rows <= tm // 2)
    kind = jnp.where(is_half, KIND_HALF,
                     jnp.where(is_comp, KIND_COMPUTE, jnp.where(is_zero, KIND_ZERO, KIND_NONE)))
    pf_chunk = jnp.where(in_blocks, jnp.minimum(p, nc - 1), nc - 1)
    pf_do = in_blocks & (p < nc)
    sched = (tile, arow, kind, rank[e], nxt[e], wrap[e], pf_chunk, pf_do,
             jnp.sum(nonempty.astype(I32)).reshape(1))
    n_steps = (total + n_tiles - n_used).astype(I32)
    return pos1.astype(I32), pos2.astype(I32), tuple(v.astype(I32) for v in sched), n_steps


def kernel(x, even_norm_mix_pre, even_w_in, even_b_forget, even_w_out, even_norm_mix_post,
           even_norm_ffn_pre, even_w_gate, even_w_up, even_w_down, even_norm_ffn_post,
           odd_norm_mix_pre, odd_w_pool, odd_pool_scale, odd_norm_mix_post, odd_norm_ffn_pre,
           odd_w_router, odd_we_gate, odd_we_up, odd_we_down, odd_norm_ffn_post):
    batch, seq, d = x.shape
    n = batch * seq
    x0 = x.reshape(n, d)
    ret_w = N_RET_HEADS * HEAD_DIM
    fox_w = N_FOX_HEADS * HEAD_DIM
    main_cols = 4 * ret_w + 3 * fox_w

    h0 = rms_cast(x0, even_norm_mix_pre[0])
    w_in_t = jnp.swapaxes(even_w_in, 1, 2)
    proj = gmm(h0, [w_in_t], tm=1024, tn=1024, n_out=main_cols, out_dtype=BF16, nc=4,
               transposed=True)
    b_forget = jnp.pad(even_b_forget[0], (0, LANES - N_FOX_HEADS)).reshape(1, LANES)
    f_logit = gmm(h0, [w_in_t], tm=512, tn=LANES, n_out=LANES, out_dtype=F32, nc=1,
                  transposed=True, w_col=lambda j: main_cols // LANES + j,
                  n_valid=N_FOX_HEADS)
    c_cols = forget_cumsum(f_logit, b_forget, batch)
    c_rows = c_cols[:, :N_FOX_HEADS].reshape(batch, seq, N_FOX_HEADS)
    c_rows = c_rows.transpose(0, 2, 1).reshape(batch, N_FOX_HEADS, 1, seq)
    cos, sin = _rope_tables(seq)
    log_gamma = jnp.log1p(-(2.0 ** (-5.0 - jnp.arange(N_RET_HEADS, dtype=F32))))
    ret = retention(proj, cos, sin, log_gamma, batch)
    fox = fox_attention(proj, c_cols, c_rows, batch)
    m = gmm([ret, fox], [even_w_out], tm=1024, tn=1024, n_out=d, out_dtype=F32, nc=8)
    x1, h1 = resid_norm(x0, m, even_norm_mix_post[0], even_norm_ffn_pre[0], "bf16")

    d_ff = even_w_gate.shape[-1]
    act = gmm(h1, [even_w_gate, even_w_up], tm=1024, tn=512, n_out=d_ff, out_dtype=BF16, nc=4)
    f = gmm(act, [even_w_down], tm=512, tn=512, n_out=d, out_dtype=F32, nc=16)
    x2, pooled = resid_norm(x1, f, even_norm_ffn_post[0], odd_norm_mix_pre[0], "pooled",
                            seq=seq)
    group = d // len(POOL_WINDOWS)
    tn_pool = 512
    per = group // tn_pool
    m2 = gmm(pooled, [odd_w_pool[0]], tm=2048, tn=tn_pool, n_out=d, out_dtype=F32, nc=2,
             k=group, scale=odd_pool_scale[0].reshape(1, d),
             a_col=lambda j: j // per, w_group=lambda j: j // per, w_col=lambda j: j % per)
    w_router = jnp.pad(odd_w_router[0], ((0, 0), (0, LANES - N_EXPERTS)))
    x3, h3, logits = resid_norm(x2, m2, odd_norm_mix_post[0], odd_norm_ffn_pre[0], "packed",
                                w_router=w_router)

    ids, gate, counts = route(logits)
    tm = MOE_TILE
    n_rows = 2 * n + N_EXPERTS * tm
    pos1, pos2, sched, n_steps = _moe_schedule(ids, counts, tm, n_rows // tm, MOE_CHUNKS)
    xs = gather_rows(h3, pos1, pos2, n_rows, d)
    d_fe = odd_we_gate.shape[-1]
    act2 = gmm(xs, [odd_we_gate[0], odd_we_up[0]], tm=tm, tn=1024, n_out=d_fe,
               out_dtype=BF16, nc=MOE_CHUNKS, sched=sched, merge_cast=False, n_steps=n_steps,
               half_tiles=True)
    o = gmm(act2, [odd_we_down[0]], tm=tm, tn=1024, n_out=d, out_dtype=F32,
            nc=MOE_CHUNKS, sched=sched, merge_cast=False, n_steps=n_steps, half_tiles=True)
    out = combine(o, pos1, pos2, gate, x3, odd_norm_ffn_post[0])
    return out.reshape(batch, seq, d)
```

```python
import functools

import jax
import jax.numpy as jnp
from jax import lax
from jax.experimental import pallas as pl
from jax.experimental.pallas import tpu as pltpu

F32 = jnp.float32
BF16 = jnp.bfloat16
U32 = jnp.uint32
I32 = jnp.int32

HEAD_DIM = 128
N_RET_HEADS = 16
N_FOX_HEADS = 16
RET_CHUNK = 128
ROPE_BASE = 10000.0
N_EXPERTS = 8
POOL_WINDOWS = (2, 4, 8, 16)
EPS = 1e-6
LANES = 128
NEG_BIG = -1e30

VMEM_LIMIT = 56 * 1024 * 1024
MOE_TILE = 256
MOE_CHUNKS = 8


def _cparams(n_axes):
    return pltpu.CompilerParams(
        dimension_semantics=("arbitrary",) * n_axes, vmem_limit_bytes=VMEM_LIMIT)


def _rms(v):
    return lax.rsqrt(jnp.mean(v * v, axis=-1, keepdims=True) + EPS)


def _sigmoid(v):
    return 1.0 / (1.0 + jnp.exp(-v))


def _rms_cast_kernel(x_ref, g_ref, o_ref):
    x = x_ref[...]
    o_ref[...] = ((x * _rms(x)) * g_ref[...]).astype(o_ref.dtype)


def rms_cast(x, g, tm=256):
    n, d = x.shape
    return pl.pallas_call(
        _rms_cast_kernel,
        grid=(n // tm,),
        in_specs=[pl.BlockSpec((tm, d), lambda i: (i, 0)),
                  pl.BlockSpec((1, d), lambda i: (0, 0))],
        out_specs=pl.BlockSpec((tm, d), lambda i: (i, 0)),
        out_shape=jax.ShapeDtypeStruct((n, d), BF16),
        compiler_params=_cparams(1),
        name="rms_cast",
    )(x, g.reshape(1, d))


def _pack_bf16_pairs(h):
    half = h.shape[1] // 2
    bits = lax.bitcast_convert_type(h.astype(BF16).astype(F32), U32)
    return (bits[:, :half] >> 16) | (bits[:, half:] & jnp.uint32(0xFFFF0000))


def _unpack_bf16_pairs(p):
    lo = lax.bitcast_convert_type(p << 16, F32).astype(BF16)
    hi = lax.bitcast_convert_type(p & jnp.uint32(0xFFFF0000), F32).astype(BF16)
    return lo, hi


def _pool_tile(h, o_ref, halo_ref, tiles_per_seq):
    tm, d = h.shape
    group = d // len(POOL_WINDOWS)
    seq_tile = pl.program_id(0) % tiles_per_seq

    @pl.when(seq_tile == 0)
    def _():
        halo_ref[...] = jnp.zeros_like(halo_ref)

    row = lax.broadcasted_iota(I32, (tm, group), 0)
    row8 = lax.broadcasted_iota(I32, (8, group), 0)
    count_cap = seq_tile * tm + row + 1
    for gi, w in enumerate(POOL_WINDOWS):
        cols = slice(gi * group, (gi + 1) * group)
        x = h[:, cols]
        acc = x
        by, level = 1, 0
        while by < w:
            prev = halo_ref[level, :, cols]
            rolled = pltpu.roll(acc, by, 0)
            prev_rolled = pltpu.roll(prev, by, 0) if by < 8 else prev
            top = jnp.where(row8 < by, prev_rolled, rolled[0:8])
            halo_ref[level, :, cols] = acc[tm - 8:tm]
            acc = acc + jnp.concatenate([top, rolled[8:]], axis=0)
            by *= 2
            level += 1
        count = jnp.minimum(count_cap, w).astype(F32)
        o_ref[:, cols] = (acc / count - x).astype(o_ref.dtype)


def _resid_norm_kernel(*refs, h_mode, router, tiles_per_seq):
    x_ref, m_ref, gp_ref = refs[:3]
    pos = 3
    gn_ref = wr_ref = None
    if h_mode is not None:
        gn_ref = refs[pos]; pos += 1
    if router:
        wr_ref = refs[pos]; pos += 1
    xo_ref = refs[pos]; pos += 1
    m = m_ref[...]
    xn = x_ref[...] + (m * _rms(m)) * gp_ref[...]
    xo_ref[...] = xn
    if h_mode is None:
        return
    h = (xn * _rms(xn)) * gn_ref[...]
    h_ref = refs[pos]; pos += 1
    if h_mode == "packed":
        packed = _pack_bf16_pairs(h)
        nb = packed.shape[1] // LANES
        for c in range(nb):
            h_ref[pl.ds(c, packed.shape[0], stride=nb), :] = packed[:, c * LANES:(c + 1) * LANES]
    elif h_mode == "pooled":
        _pool_tile(h, h_ref, refs[-1], tiles_per_seq)
    else:
        h_ref[...] = h.astype(h_ref.dtype)
    if router:
        refs[pos][...] = jnp.dot(h, wr_ref[...], preferred_element_type=F32,
                                 precision=lax.Precision.HIGHEST)


def resid_norm(x, m, g_post, g_next=None, h_mode=None, w_router=None, seq=None, tm=256):
    n, d = x.shape
    row = pl.BlockSpec((tm, d), lambda i: (i, 0))
    vec = pl.BlockSpec((1, d), lambda i: (0, 0))
    args = [x, m, g_post.reshape(1, d)]
    in_specs = [row, row, vec]
    out_shape = [jax.ShapeDtypeStruct((n, d), F32)]
    out_specs = [row]
    if h_mode is not None:
        args.append(g_next.reshape(1, d)); in_specs.append(vec)
        if h_mode == "packed":
            nb = d // 2 // LANES
            out_shape.append(jax.ShapeDtypeStruct((n * nb, LANES), U32))
            out_specs.append(pl.BlockSpec((tm * nb, LANES), lambda i: (i, 0)))
        else:
            out_shape.append(jax.ShapeDtypeStruct((n, d), BF16))
            out_specs.append(row)
    scratch = []
    if h_mode == "pooled":
        n_levels = len(POOL_WINDOWS)
        scratch.append(pltpu.VMEM((n_levels, 8, d), F32))
    router = w_router is not None
    if router:
        args.append(w_router)
        in_specs.append(pl.BlockSpec((d, LANES), lambda i: (0, 0)))
        out_shape.append(jax.ShapeDtypeStruct((n, LANES), F32))
        out_specs.append(pl.BlockSpec((tm, LANES), lambda i: (i, 0)))
    return pl.pallas_call(
        functools.partial(_resid_norm_kernel, h_mode=h_mode, router=router,
                          tiles_per_seq=None if seq is None else seq // tm),
        grid=(n // tm,),
        in_specs=in_specs, out_specs=out_specs, out_shape=out_shape,
        scratch_shapes=scratch,
        compiler_params=_cparams(1),
        name="resid_norm",
    )(*args)


N_SCHED = 9
KIND_NONE, KIND_COMPUTE, KIND_ZERO, KIND_HALF = 0, 1, 2, 3


def _gmm_kernel(tile_ref, arow_ref, kind_ref, blk_ref, pfg_ref, pfw_ref, pfc_ref, pfd_ref,
                nblk_ref, *refs, n_a, n_w, n_col, has_scale, transposed, n_valid, merge_cast,
                half_tiles):
    del tile_ref, arow_ref, pfg_ref
    a_refs = refs[:n_a]
    c_refs = refs[n_a:n_a + n_w]
    pos = n_a + n_w
    s_ref = None
    if has_scale:
        s_ref = refs[pos]; pos += 1
    o_ref = refs[pos]; pos += 1
    slots = (refs[pos:pos + n_w], refs[pos + n_w:pos + 2 * n_w])
    jj = pl.program_id(0)
    s = pl.program_id(1)
    parity = (jj * nblk_ref[0] + blk_ref[s] + nblk_ref[0]) % 2
    rows_per_chunk = c_refs[0].shape[1]
    tcol = jj - 1 + pfw_ref[s]
    chunk_due = (pfd_ref[s] == 1) & (tcol >= 0) & (tcol < n_col)
    kind = kind_ref[s]
    computing = ((kind == KIND_COMPUTE) | (kind == KIND_HALF)) & (jj >= 1)
    tm = o_ref.shape[0]

    def cast_chunk(nxt):
        r = pl.multiple_of(pfc_ref[s] * rows_per_chunk, rows_per_chunk)
        for c_ref, wbf in zip(c_refs, nxt):
            wbf[pl.ds(r, rows_per_chunk), :] = c_ref[0].astype(BF16)

    def matmul(wbf, rows):
        if transposed:
            return lax.dot_general(a_refs[0][0:rows, :], wbf[...], (((1,), (1,)), ((), ())),
                                   preferred_element_type=F32)
        acc = None
        off = 0
        for a_ref in a_refs:
            ka = a_ref.shape[1]
            part = jnp.dot(a_ref[0:rows, :], wbf[off:off + ka, :], preferred_element_type=F32)
            acc = part if acc is None else acc + part
            off += ka
        return acc

    def compute(cur, nxt, rows):
        if merge_cast:
            cast_chunk(nxt)
        acc = matmul(cur[0], rows)
        if n_w == 2:
            acc = (acc * _sigmoid(acc)) * matmul(cur[1], rows)
        if has_scale:
            acc = acc * s_ref[...]
        if n_valid is not None:
            col = lax.broadcasted_iota(I32, acc.shape, 1) + (jj - 1) * acc.shape[1]
            acc = jnp.where(col < n_valid, acc, 0.0)
        o_ref[0:rows, :] = acc.astype(o_ref.dtype)
        if rows < tm:
            o_ref[rows:tm, :] = jnp.zeros((tm - rows, o_ref.shape[1]), o_ref.dtype)

    heights = ((KIND_COMPUTE, tm), (KIND_HALF, tm // 2)) if half_tiles else ((KIND_COMPUTE, tm),)
    for p in (0, 1):
        cur, nxt = slots[p], slots[1 - p]
        for which, rows in heights:
            pl.when((kind == which) & (jj >= 1) & (parity == p))(
                functools.partial(compute, cur, nxt, rows))
        cast_here = chunk_due & jnp.logical_not(computing) if merge_cast else chunk_due
        pl.when(cast_here & (parity == p))(functools.partial(cast_chunk, nxt))

    @pl.when((kind_ref[s] == KIND_ZERO) & (jj >= 1))
    def _():
        o_ref[...] = jnp.zeros_like(o_ref)


def _dense_schedule(n_tiles, nc):
    t = jnp.arange(n_tiles, dtype=I32)
    zeros = jnp.zeros((n_tiles,), I32)
    ones = jnp.ones((n_tiles,), I32)
    return (t, t, ones, zeros, zeros, ones, jnp.minimum(t, nc - 1), (t < nc).astype(I32),
            jnp.array([1, 0], I32))


def gmm(a, ws, *, tm, tn, n_out, out_dtype, nc, sched=None, k=None, scale=None,
        a_col=None, w_group=None, w_col=None, transposed=False, n_valid=None,
        merge_cast=True, n_steps=None, half_tiles=False):
    a_parts = list(a) if isinstance(a, (list, tuple)) else [a]
    rows = a_parts[0].shape[0]
    if k is None:
        k = sum(p.shape[1] for p in a_parts)
    if sched is None:
        sched = _dense_schedule(rows // tm, nc)
    if n_steps is None:
        n_steps = sched[0].shape[0]
    n_col = pl.cdiv(n_out, tn)
    a_col = a_col or (lambda j: 0)
    w_col = w_col or (lambda j: j)

    def cur_col(jj):
        return jnp.maximum(jj - 1, 0)

    def a_map(jj, s, tile, arow, *_):
        return (jnp.where(jj == 0, arow[0], arow[s]), a_col(cur_col(jj)))

    def o_map(jj, s, tile, *_):
        return (jnp.where(jj == 0, tile[0], tile[s]), cur_col(jj))

    def c_map(jj, s, tile, arow, kind, blk, pfg, pfw, pfc, pfd, meta):
        tcol = jj - 1 + pfw[s]
        col = jnp.clip(tcol, 0, n_col - 1)
        idle = tcol < 0
        chunk = jnp.where(idle, 0, pfc[s])
        group = jnp.where(idle, meta[1], pfg[s]) if w_group is None else w_group(col)
        if transposed:
            return (group, w_col(col) * nc + chunk, 0)
        return (group, chunk, w_col(col))

    in_specs = [pl.BlockSpec((tm, k if len(a_parts) == 1 else p.shape[1]), a_map)
                for p in a_parts]
    if transposed:
        assert len(a_parts) == 1 and tn % nc == 0
        chunk, wbf_shape = (1, tn // nc, k), (tn, k)
    else:
        assert k % nc == 0
        chunk, wbf_shape = (1, k // nc, tn), (k, tn)
    in_specs += [pl.BlockSpec(chunk, c_map) for _ in ws]
    args = [*a_parts, *ws]
    if scale is not None:
        in_specs.append(pl.BlockSpec((1, tn), lambda jj, s, *_: (0, cur_col(jj))))
        args.append(scale)
    grid_spec = pltpu.PrefetchScalarGridSpec(
        num_scalar_prefetch=N_SCHED,
        grid=(n_col + 1, n_steps),
        in_specs=in_specs,
        out_specs=pl.BlockSpec((tm, tn), o_map),
        scratch_shapes=[pltpu.VMEM(wbf_shape, BF16) for _ in range(2 * len(ws))],
    )
    return pl.pallas_call(
        functools.partial(_gmm_kernel, n_a=len(a_parts), n_w=len(ws), n_col=n_col,
                          has_scale=scale is not None, transposed=transposed,
                          n_valid=n_valid, merge_cast=merge_cast, half_tiles=half_tiles),
        grid_spec=grid_spec,
        out_shape=jax.ShapeDtypeStruct((rows, n_out), out_dtype),
        compiler_params=_cparams(2),
        name="gmm",
    )(*sched, *args)


def _forget_cumsum_kernel(f_ref, b_ref, o_ref):
    x = f_ref[...] + b_ref[...]
    ls = jnp.minimum(x, 0.0) - jnp.log1p(jnp.exp(-jnp.abs(x)))
    s = ls.shape[0]
    row = lax.broadcasted_iota(I32, ls.shape, 0)
    sh = 1
    while sh < s:
        ls = ls + jnp.where(row >= sh, pltpu.roll(ls, sh, 0), 0.0)
        sh *= 2
    o_ref[...] = ls


def forget_cumsum(f, b, batch):
    n, w = f.shape
    s = n // batch
    return pl.pallas_call(
        _forget_cumsum_kernel,
        grid=(batch,),
        in_specs=[pl.BlockSpec((s, w), lambda i: (i, 0)),
                  pl.BlockSpec((1, w), lambda i: (0, 0))],
        out_specs=pl.BlockSpec((s, w), lambda i: (i, 0)),
        out_shape=jax.ShapeDtypeStruct((n, w), F32),
        compiler_params=_cparams(1),
        name="forget_cumsum",
    )(f, b)


def _retention_kernel(lg_ref, q_ref, k_ref, v_ref, g_ref, cos_ref, sin_ref, o_ref):
    c = RET_CHUNK
    d = HEAD_DIM
    s = q_ref.shape[0]
    lg = lg_ref[pl.program_id(1)]
    ii = lax.broadcasted_iota(I32, (c, c), 0)
    jj = lax.broadcasted_iota(I32, (c, c), 1)
    diff = (ii - jj).astype(F32)
    decay_in = jnp.where(diff >= 0, jnp.exp(lg * jnp.maximum(diff, 0.0)), 0.0)
    jc = lax.broadcasted_iota(I32, (c, 1), 0).astype(F32)
    zeta = jnp.exp(lg * (c - 1 - jc))
    q_decay = jnp.exp(lg * (jc + 1.0))
    gamma_c = jnp.exp(jnp.full((1, 1), lg, F32) * c)
    scale = d ** -0.5
    nt = (((1,), (1,)), ((), ()))
    tn = (((0,), (0,)), ((), ()))
    state = jnp.zeros((d, d), F32)
    for n in range(s // c):
        sl = pl.ds(n * c, c)
        cos = cos_ref[sl, :]
        sin = sin_ref[sl, :]
        q = q_ref[sl, :].astype(F32)
        k = k_ref[sl, :].astype(F32)
        v = v_ref[sl, :]
        qr = q * cos + pltpu.roll(q, d // 2, 1) * sin
        kr = (k * cos + pltpu.roll(k, d // 2, 1) * sin) * scale
        scores = lax.dot_general(qr.astype(BF16), kr.astype(BF16), nt,
                                 preferred_element_type=F32) * decay_in
        inner = jnp.dot(scores.astype(BF16), v, preferred_element_type=F32)
        cross = jnp.dot((qr * q_decay).astype(BF16), state.astype(BF16),
                        preferred_element_type=F32)
        kv = lax.dot_general((kr * zeta).astype(BF16), v, tn, preferred_element_type=F32)
        state = gamma_c * state + kv
        o = inner + cross
        xc = o - jnp.mean(o, axis=-1, keepdims=True)
        y = xc * lax.rsqrt(jnp.mean(xc * xc, axis=-1, keepdims=True) + EPS)
        g = g_ref[sl, :].astype(F32)
        o_ref[sl, :] = ((g * _sigmoid(g)) * y).astype(o_ref.dtype)


def retention(proj, cos, sin, log_gamma, batch):
    n = proj.shape[0]
    s = n // batch
    d = HEAD_DIM
    hh = N_RET_HEADS

    def col(off):
        return pl.BlockSpec((s, d), lambda b, h, lg: (b, off + h))

    tab = pl.BlockSpec((s, d), lambda b, h, lg: (0, 0))
    grid_spec = pltpu.PrefetchScalarGridSpec(
        num_scalar_prefetch=1,
        grid=(batch, hh),
        in_specs=[col(0), col(hh), col(2 * hh), col(3 * hh), tab, tab],
        out_specs=pl.BlockSpec((s, d), lambda b, h, lg: (b, h)),
    )
    return pl.pallas_call(
        _retention_kernel,
        grid_spec=grid_spec,
        out_shape=jax.ShapeDtypeStruct((n, hh * d), BF16),
        compiler_params=_cparams(2),
        name="retention",
    )(log_gamma, proj, proj, proj, proj, cos, sin)


def _fox_kernel(q_ref, k_ref, v_ref, ccol_ref, crow_ref, o_ref, *, tq):
    s, d = q_ref.shape
    h = pl.program_id(1)
    scale = d ** -0.5
    nt = (((1,), (1,)), ((), ()))
    lane = lax.broadcasted_iota(I32, (tq, LANES), 1)
    rr = lax.broadcasted_iota(I32, (tq, tq), 0)
    cc = lax.broadcasted_iota(I32, (tq, tq), 1)

    for qi in range(s // tq):
        qs = pl.ds(qi * tq, tq)
        q = q_ref[qs, :]
        cq = jnp.sum(jnp.where(lane == h, ccol_ref[qs, :], 0.0), axis=-1, keepdims=True)

        def scores(kb):
            ks = pl.ds(pl.multiple_of(kb * tq, tq), tq)
            sc = lax.dot_general(q, k_ref[ks, :], nt, preferred_element_type=F32) * scale
            return sc + cq - crow_ref[:, ks], v_ref[ks, :]

        def update(carry, sc, v):
            m, l, acc = carry
            m_new = jnp.maximum(m, jnp.max(sc, axis=-1, keepdims=True))
            alpha = jnp.exp(m - m_new)
            p = jnp.exp(sc - m_new)
            l = alpha * l + jnp.sum(p, axis=-1, keepdims=True)
            acc = alpha * acc + jnp.dot(p.astype(BF16), v, preferred_element_type=F32)
            return m_new, l, acc

        def body(kb, carry):
            sc, v = scores(kb)
            return update(carry, sc, v)

        carry = (jnp.full((tq, 1), NEG_BIG, F32), jnp.zeros((tq, 1), F32),
                 jnp.zeros((tq, d), F32))
        carry = lax.fori_loop(0, qi, body, carry, unroll=True)
        sc, v = scores(qi)
        _, l, acc = update(carry, jnp.where(rr >= cc, sc, NEG_BIG), v)
        o_ref[qs, :] = (acc / l).astype(o_ref.dtype)


def fox_attention(proj, c_cols, c_rows, batch, tq=256):
    n = proj.shape[0]
    s = n // batch
    d = HEAD_DIM
    base = 4 * N_RET_HEADS
    hh = N_FOX_HEADS

    def col(off):
        return pl.BlockSpec((s, d), lambda b, h: (b, off + h))

    return pl.pallas_call(
        functools.partial(_fox_kernel, tq=tq),
        grid=(batch, hh),
        in_specs=[col(base), col(base + hh), col(base + 2 * hh),
                  pl.BlockSpec((s, LANES), lambda b, h: (b, 0)),
                  pl.BlockSpec((None, None, 1, s), lambda b, h: (b, h, 0, 0))],
        out_specs=pl.BlockSpec((s, d), lambda b, h: (b, h)),
        out_shape=jax.ShapeDtypeStruct((n, hh * d), BF16),
        compiler_params=_cparams(2),
        name="fox_attention",
    )(proj, proj, proj, c_cols, c_rows)


def _route_kernel(lg_ref, ids_ref, gate_ref, cnt_ref, carry_ref):
    tt = lg_ref.shape[0]

    @pl.when(pl.program_id(0) == 0)
    def _():
        carry_ref[...] = jnp.zeros_like(carry_ref)

    lane = lax.broadcasted_iota(I32, (tt, LANES), 1)
    lane_f = lane.astype(F32)
    x = jnp.where(lane < N_EXPERTS, lg_ref[...], -jnp.inf)
    v1 = jnp.max(x, axis=-1, keepdims=True)
    i1 = jnp.min(jnp.where(x == v1, lane_f, float(LANES)), axis=-1, keepdims=True)
    x2 = jnp.where(lane_f == i1, -jnp.inf, x)
    v2 = jnp.max(x2, axis=-1, keepdims=True)
    i2 = jnp.min(jnp.where(x2 == v2, lane_f, float(LANES)), axis=-1, keepdims=True)
    e = jnp.exp(v2 - v1)
    w1 = 1.0 / (1.0 + e)
    w2 = e / (1.0 + e)
    sel1 = lane_f == i1
    sel2 = lane_f == i2
    onehot = jnp.where(sel1, 1.0, jnp.where(sel2, 1.0, 0.0))
    rr = lax.broadcasted_iota(I32, (tt, tt), 0)
    cc = lax.broadcasted_iota(I32, (tt, tt), 1)
    tri = jnp.where(rr > cc, 1.0, 0.0).astype(BF16)
    before = jnp.dot(tri, onehot.astype(BF16), preferred_element_type=F32) + carry_ref[0:1, :]
    r1 = jnp.sum(jnp.where(sel1, before, 0.0), axis=-1, keepdims=True)
    r2 = jnp.sum(jnp.where(sel2, before, 0.0), axis=-1, keepdims=True)
    ids = jnp.where(lane == 0, i1, jnp.where(lane == 1, i2,
                    jnp.where(lane == 2, r1, jnp.where(lane == 3, r2, 0.0))))
    ids_ref[...] = ids.astype(I32)
    gate_ref[...] = jnp.where(lane == 0, w1, jnp.where(lane == 1, w2, 0.0))
    total = carry_ref[0:1, :] + jnp.sum(onehot, axis=0, keepdims=True)
    carry_ref[...] = jnp.broadcast_to(total, carry_ref.shape)
    cnt_ref[...] = jnp.broadcast_to(total, cnt_ref.shape).astype(I32)


def route(logits, tt=256):
    n = logits.shape[0]
    row = pl.BlockSpec((tt, LANES), lambda i: (i, 0))
    return pl.pallas_call(
        _route_kernel,
        grid=(n // tt,),
        in_specs=[row],
        out_specs=[row, row, pl.BlockSpec((8, LANES), lambda i: (0, 0))],
        out_shape=[jax.ShapeDtypeStruct((n, LANES), I32),
                   jax.ShapeDtypeStruct((n, LANES), F32),
                   jax.ShapeDtypeStruct((8, LANES), I32)],
        scratch_shapes=[pltpu.VMEM((8, LANES), F32)],
        compiler_params=_cparams(1),
        name="route",
    )(logits)


def _gather_kernel(pos1_ref, pos2_ref, h_hbm, o_ref, src_ref, buf_ref, sems, *,
                   n_tokens, nb):
    tg = o_ref.shape[0]
    i = pl.program_id(0)
    n_tiles = pl.num_programs(0) - 1

    @pl.when(i == 0)
    def _():
        def clear(p, carry):
            src_ref[p] = 0
            return carry
        lax.fori_loop(0, src_ref.shape[0], clear, 0, unroll=8)

        def place(t, carry):
            src_ref[pos1_ref[t]] = t
            src_ref[pos2_ref[t]] = t
            return carry
        lax.fori_loop(0, n_tokens, place, 0, unroll=4)

    def slab_copy(tile, slot, r):
        src = pl.multiple_of(src_ref[tile * tg + r] * nb, nb)
        dst = pl.multiple_of(r * nb, nb)
        return pltpu.make_async_copy(
            h_hbm.at[pl.ds(src, nb), :], buf_ref.at[slot, pl.ds(dst, nb), :], sems.at[slot])

    @pl.when(i < n_tiles)
    def _():
        def issue(r, carry):
            slab_copy(i, i % 2, r).start()
            return carry
        lax.fori_loop(0, tg, issue, 0, unroll=4)

    @pl.when(i > 0)
    def _():
        slot = (i - 1) % 2

        def drain(r, carry):
            slab_copy(i - 1, slot, r).wait()
            return carry
        lax.fori_loop(0, tg, drain, 0, unroll=4)
        half = nb * LANES
        for c in range(nb):
            lo, hi = _unpack_bf16_pairs(buf_ref[slot, pl.ds(c, tg, stride=nb), :])
            o_ref[:, c * LANES:(c + 1) * LANES] = lo
            o_ref[:, half + c * LANES:half + (c + 1) * LANES] = hi


def gather_rows(h_packed, pos1, pos2, n_rows, d, tg=256):
    nb = d // 2 // LANES
    n = h_packed.shape[0] // nb
    width = 2 * nb * LANES
    n_tiles = n_rows // tg
    grid_spec = pltpu.PrefetchScalarGridSpec(
        num_scalar_prefetch=2,
        grid=(n_tiles + 1,),
        in_specs=[pl.BlockSpec(memory_space=pl.ANY)],
        out_specs=pl.BlockSpec((tg, width), lambda i, p1, p2: (jnp.maximum(i - 1, 0), 0)),
        scratch_shapes=[pltpu.SMEM((n_rows,), I32), pltpu.VMEM((2, tg * nb, LANES), U32),
                        pltpu.SemaphoreType.DMA((2,))],
    )
    return pl.pallas_call(
        functools.partial(_gather_kernel, n_tokens=n, nb=nb),
        grid_spec=grid_spec,
        out_shape=jax.ShapeDtypeStruct((n_rows, width), BF16),
        compiler_params=_cparams(1),
        name="gather_rows",
    )(pos1, pos2, h_packed)


def _combine_kernel(pos1_ref, pos2_ref, o_hbm, gate_ref, x_ref, g_ref, out_ref, buf_ref, sems):
    tt = x_ref.shape[0]
    i = pl.program_id(0)
    n_tiles = pl.num_programs(0) - 1

    def row_copy(tile, slot, r, which, pos_ref):
        return pltpu.make_async_copy(
            o_hbm.at[pl.ds(pos_ref[tile * tt + r], 1), :],
            buf_ref.at[slot, which, pl.ds(r, 1), :], sems.at[slot])

    @pl.when(i < n_tiles)
    def _():
        def issue(r, carry):
            row_copy(i, i % 2, r, 0, pos1_ref).start()
            row_copy(i, i % 2, r, 1, pos2_ref).start()
            return carry
        lax.fori_loop(0, tt, issue, 0, unroll=4)

    @pl.when(i > 0)
    def _():
        slot = (i - 1) % 2

        def drain(r, carry):
            row_copy(i - 1, slot, r, 0, pos1_ref).wait()
            row_copy(i - 1, slot, r, 1, pos2_ref).wait()
            return carry
        lax.fori_loop(0, tt, drain, 0, unroll=4)
        gate = gate_ref[...]
        y = gate[:, 0:1] * buf_ref[slot, 0] + gate[:, 1:2] * buf_ref[slot, 1]
        out_ref[...] = x_ref[...] + (y * _rms(y)) * g_ref[...]


def combine(o, pos1, pos2, gate, x, g, tt=128):
    n, d = x.shape
    row = lambda i, p1, p2: (jnp.maximum(i - 1, 0), 0)
    grid_spec = pltpu.PrefetchScalarGridSpec(
        num_scalar_prefetch=2,
        grid=(n // tt + 1,),
        in_specs=[pl.BlockSpec(memory_space=pl.ANY),
                  pl.BlockSpec((tt, LANES), row),
                  pl.BlockSpec((tt, d), row),
                  pl.BlockSpec((1, d), lambda i, p1, p2: (0, 0))],
        out_specs=pl.BlockSpec((tt, d), row),
        scratch_shapes=[pltpu.VMEM((2, 2, tt, d), F32), pltpu.SemaphoreType.DMA((2,))],
    )
    return pl.pallas_call(
        _combine_kernel,
        grid_spec=grid_spec,
        out_shape=jax.ShapeDtypeStruct((n, d), F32),
        compiler_params=_cparams(1),
        name="combine",
    )(pos1, pos2, o, gate, x, g.reshape(1, d))


def _rope_tables(s):
    half = HEAD_DIM // 2
    inv = ROPE_BASE ** (-jnp.arange(half, dtype=F32) / half)
    ang = jnp.arange(s, dtype=F32)[:, None] * inv[None, :]
    cos, sin = jnp.cos(ang), jnp.sin(ang)
    return jnp.concatenate([cos, cos], axis=-1), jnp.concatenate([-sin, sin], axis=-1)


def _moe_schedule(ids, counts, tm, n_tiles, nc):
    ne = N_EXPERTS
    counts = counts[0, :ne]
    tiles = (counts + tm - 1) // tm
    nonempty = tiles > 0
    tile_end = jnp.cumsum(tiles)
    tile_start = tile_end - tiles
    pos1 = tile_start[ids[:, 0]] * tm + ids[:, 2]
    pos2 = tile_start[ids[:, 1]] * tm + ids[:, 3]

    steps = jnp.where(nonempty, jnp.maximum(tiles, nc), 0)
    step_end = jnp.cumsum(steps)
    step_start = step_end - steps
    total, n_used = step_end[-1], tile_end[-1]
    ar = jnp.arange(ne, dtype=I32)
    first_e = jnp.min(jnp.where(nonempty, ar, ne))
    last_e = jnp.max(jnp.where(nonempty, ar, -1))
    later = (ar[None, :] > ar[:, None]) & nonempty[None, :]
    nxt = jnp.min(jnp.where(later, ar[None, :], ne), axis=1)
    wrap = nxt == ne
    nxt = jnp.where(wrap, first_e, nxt)
    rank = jnp.cumsum(nonempty.astype(I32)) - 1

    s = jnp.arange(n_tiles + ne * (nc - 1), dtype=I32)
    in_blocks = s < total
    e = jnp.minimum(jnp.sum((s[:, None] >= step_end[None, :]).astype(I32), axis=1), ne - 1)
    e = jnp.where(in_blocks, e, last_e)
    p = s - step_start[e]
    is_comp = in_blocks & (p < tiles[e])
    z = s - total
    is_zero = (~in_blocks) & (z < n_tiles - n_used)
    group_last = tile_start[e] + tiles[e] - 1
    tile = jnp.where(is_comp, tile_start[e] + p,
                     jnp.where(is_zero, n_used + z, jnp.where(in_blocks, group_last, n_tiles - 1)))
    arow = jnp.where(is_comp, tile, jnp.where(in_blocks, group_last, n_used - 1))
    tail_rows = counts[e] - (tiles[e] - 1) * tm
    is_half = is_comp & (p == tiles[e] - 1) & (tail_rows <= tm // 2)
    kind = jnp.where(is_half, KIND_HALF,
                     jnp.where(is_comp, KIND_COMPUTE, jnp.where(is_zero, KIND_ZERO, KIND_NONE)))
    pf_chunk = jnp.where(in_blocks, jnp.minimum(p, nc - 1), nc - 1)
    pf_do = in_blocks & (p < nc)
    sched = (tile, arow, kind, rank[e], nxt[e], wrap[e], pf_chunk, pf_do,
             jnp.stack([jnp.sum(nonempty.astype(I32)), first_e]))
    n_steps = (total + n_tiles - n_used).astype(I32)
    return pos1.astype(I32), pos2.astype(I32), tuple(v.astype(I32) for v in sched), n_steps


def kernel(x, even_norm_mix_pre, even_w_in, even_b_forget, even_w_out, even_norm_mix_post,
           even_norm_ffn_pre, even_w_gate, even_w_up, even_w_down, even_norm_ffn_post,
           odd_norm_mix_pre, odd_w_pool, odd_pool_scale, odd_norm_mix_post, odd_norm_ffn_pre,
           odd_w_router, odd_we_gate, odd_we_up, odd_we_down, odd_norm_ffn_post):
    batch, seq, d = x.shape
    n = batch * seq
    x0 = x.reshape(n, d)
    ret_w = N_RET_HEADS * HEAD_DIM
    fox_w = N_FOX_HEADS * HEAD_DIM
    main_cols = 4 * ret_w + 3 * fox_w

    h0 = rms_cast(x0, even_norm_mix_pre[0])
    w_in_t = jnp.swapaxes(even_w_in, 1, 2)
    proj = gmm(h0, [w_in_t], tm=1024, tn=1024, n_out=main_cols, out_dtype=BF16, nc=4,
               transposed=True)
    b_forget = jnp.pad(even_b_forget[0], (0, LANES - N_FOX_HEADS)).reshape(1, LANES)
    f_logit = gmm(h0, [w_in_t], tm=512, tn=LANES, n_out=LANES, out_dtype=F32, nc=1,
                  transposed=True, w_col=lambda j: main_cols // LANES + j,
                  n_valid=N_FOX_HEADS)
    c_cols = forget_cumsum(f_logit, b_forget, batch)
    c_rows = c_cols[:, :N_FOX_HEADS].reshape(batch, seq, N_FOX_HEADS)
    c_rows = c_rows.transpose(0, 2, 1).reshape(batch, N_FOX_HEADS, 1, seq)
    cos, sin = _rope_tables(seq)
    log_gamma = jnp.log1p(-(2.0 ** (-5.0 - jnp.arange(N_RET_HEADS, dtype=F32))))
    ret = retention(proj, cos, sin, log_gamma, batch)
    fox = fox_attention(proj, c_cols, c_rows, batch)
    m = gmm([ret, fox], [even_w_out], tm=1024, tn=1024, n_out=d, out_dtype=F32, nc=8)
    x1, h1 = resid_norm(x0, m, even_norm_mix_post[0], even_norm_ffn_pre[0], "bf16")

    d_ff = even_w_gate.shape[-1]
    act = gmm(h1, [even_w_gate, even_w_up], tm=1024, tn=512, n_out=d_ff, out_dtype=BF16, nc=4)
    f = gmm(act, [even_w_down], tm=512, tn=512, n_out=d, out_dtype=F32, nc=16)
    x2, pooled = resid_norm(x1, f, even_norm_ffn_post[0], odd_norm_mix_pre[0], "pooled",
                            seq=seq)
    group = d // len(POOL_WINDOWS)
    tn_pool = 512
    per = group // tn_pool
    m2 = gmm(pooled, [odd_w_pool[0]], tm=2048, tn=tn_pool, n_out=d, out_dtype=F32, nc=2,
             k=group, scale=odd_pool_scale[0].reshape(1, d),
             a_col=lambda j: j // per, w_group=lambda j: j // per, w_col=lambda j: j % per)
    w_router = jnp.pad(odd_w_router[0], ((0, 0), (0, LANES - N_EXPERTS)))
    x3, h3, logits = resid_norm(x2, m2, odd_norm_mix_post[0], odd_norm_ffn_pre[0], "packed",
                                w_router=w_router)

    ids, gate, counts = route(logits)
    tm = MOE_TILE
    n_rows = 2 * n + N_EXPERTS * tm
    pos1, pos2, sched, n_steps = _moe_schedule(ids, counts, tm, n_rows // tm, MOE_CHUNKS)
    xs = gather_rows(h3, pos1, pos2, n_rows, d)
    d_fe = odd_we_gate.shape[-1]
    act2 = gmm(xs, [odd_we_gate[0], odd_we_up[0]], tm=tm, tn=1024, n_out=d_fe,
               out_dtype=BF16, nc=MOE_CHUNKS, sched=sched, merge_cast=False, n_steps=n_steps,
               half_tiles=True)
    o = gmm(act2, [odd_we_down[0]], tm=tm, tn=1024, n_out=d, out_dtype=F32,
            nc=MOE_CHUNKS, sched=sched, merge_cast=False, n_steps=n_steps, half_tiles=True)
    out = combine(o, pos1, pos2, gate, x3, odd_norm_ffn_post[0])
    return out.reshape(batch, seq, d)
```

```python
import functools

import jax
import jax.numpy as jnp
from jax import lax
from jax.experimental import pallas as pl
from jax.experimental.pallas import tpu as pltpu

F32 = jnp.float32
BF16 = jnp.bfloat16
U32 = jnp.uint32
I32 = jnp.int32

HEAD_DIM = 128
N_RET_HEADS = 16
N_FOX_HEADS = 16
RET_CHUNK = 128
ROPE_BASE = 10000.0
N_EXPERTS = 8
POOL_WINDOWS = (2, 4, 8, 16)
EPS = 1e-6
LANES = 128
NEG_BIG = -1e30

VMEM_LIMIT = 56 * 1024 * 1024
MOE_TILE = 256
MOE_CHUNKS = 8


def _cparams(n_axes):
    return pltpu.CompilerParams(
        dimension_semantics=("arbitrary",) * n_axes, vmem_limit_bytes=VMEM_LIMIT)


def _rms(v):
    return lax.rsqrt(jnp.mean(v * v, axis=-1, keepdims=True) + EPS)


def _sigmoid(v):
    return 1.0 / (1.0 + jnp.exp(-v))


def _rms_cast_kernel(x_ref, g_ref, o_ref):
    x = x_ref[...]
    o_ref[...] = ((x * _rms(x)) * g_ref[...]).astype(o_ref.dtype)


def rms_cast(x, g, tm=256):
    n, d = x.shape
    return pl.pallas_call(
        _rms_cast_kernel,
        grid=(n // tm,),
        in_specs=[pl.BlockSpec((tm, d), lambda i: (i, 0)),
                  pl.BlockSpec((1, d), lambda i: (0, 0))],
        out_specs=pl.BlockSpec((tm, d), lambda i: (i, 0)),
        out_shape=jax.ShapeDtypeStruct((n, d), BF16),
        compiler_params=_cparams(1),
        name="rms_cast",
    )(x, g.reshape(1, d))


def _pack_bf16_pairs(h):
    half = h.shape[1] // 2
    bits = lax.bitcast_convert_type(h.astype(BF16).astype(F32), U32)
    return (bits[:, :half] >> 16) | (bits[:, half:] & jnp.uint32(0xFFFF0000))


def _unpack_bf16_pairs(p):
    lo = lax.bitcast_convert_type(p << 16, F32).astype(BF16)
    hi = lax.bitcast_convert_type(p & jnp.uint32(0xFFFF0000), F32).astype(BF16)
    return lo, hi


def _pool_tile(h, o_ref, halo_ref, tiles_per_seq):
    tm, d = h.shape
    group = d // len(POOL_WINDOWS)
    seq_tile = pl.program_id(0) % tiles_per_seq

    @pl.when(seq_tile == 0)
    def _():
        halo_ref[...] = jnp.zeros_like(halo_ref)

    row = lax.broadcasted_iota(I32, (tm, group), 0)
    row8 = lax.broadcasted_iota(I32, (8, group), 0)
    count_cap = seq_tile * tm + row + 1
    for gi, w in enumerate(POOL_WINDOWS):
        cols = slice(gi * group, (gi + 1) * group)
        x = h[:, cols]
        acc = x
        by, level = 1, 0
        while by < w:
            prev = halo_ref[level, :, cols]
            rolled = pltpu.roll(acc, by, 0)
            prev_rolled = pltpu.roll(prev, by, 0) if by < 8 else prev
            top = jnp.where(row8 < by, prev_rolled, rolled[0:8])
            halo_ref[level, :, cols] = acc[tm - 8:tm]
            acc = acc + jnp.concatenate([top, rolled[8:]], axis=0)
            by *= 2
            level += 1
        count = jnp.minimum(count_cap, w).astype(F32)
        o_ref[:, cols] = (acc / count - x).astype(o_ref.dtype)


def _resid_norm_kernel(*refs, h_mode, router, tiles_per_seq):
    x_ref, m_ref, gp_ref = refs[:3]
    pos = 3
    gn_ref = wr_ref = None
    if h_mode is not None:
        gn_ref = refs[pos]; pos += 1
    if router:
        wr_ref = refs[pos]; pos += 1
    xo_ref = refs[pos]; pos += 1
    m = m_ref[...]
    xn = x_ref[...] + (m * _rms(m)) * gp_ref[...]
    xo_ref[...] = xn
    if h_mode is None:
        return
    h = (xn * _rms(xn)) * gn_ref[...]
    h_ref = refs[pos]; pos += 1
    if h_mode == "packed":
        packed = _pack_bf16_pairs(h)
        nb = packed.shape[1] // LANES
        for c in range(nb):
            h_ref[pl.ds(c, packed.shape[0], stride=nb), :] = packed[:, c * LANES:(c + 1) * LANES]
    elif h_mode == "pooled":
        _pool_tile(h, h_ref, refs[-1], tiles_per_seq)
    else:
        h_ref[...] = h.astype(h_ref.dtype)
    if router:
        refs[pos][...] = jnp.dot(h, wr_ref[...], preferred_element_type=F32,
                                 precision=lax.Precision.HIGHEST)


def resid_norm(x, m, g_post, g_next=None, h_mode=None, w_router=None, seq=None, tm=256):
    n, d = x.shape
    row = pl.BlockSpec((tm, d), lambda i: (i, 0))
    vec = pl.BlockSpec((1, d), lambda i: (0, 0))
    args = [x, m, g_post.reshape(1, d)]
    in_specs = [row, row, vec]
    out_shape = [jax.ShapeDtypeStruct((n, d), F32)]
    out_specs = [row]
    if h_mode is not None:
        args.append(g_next.reshape(1, d)); in_specs.append(vec)
        if h_mode == "packed":
            nb = d // 2 // LANES
            out_shape.append(jax.ShapeDtypeStruct((n * nb, LANES), U32))
            out_specs.append(pl.BlockSpec((tm * nb, LANES), lambda i: (i, 0)))
        else:
            out_shape.append(jax.ShapeDtypeStruct((n, d), BF16))
            out_specs.append(row)
    scratch = []
    if h_mode == "pooled":
        n_levels = len(POOL_WINDOWS)
        scratch.append(pltpu.VMEM((n_levels, 8, d), F32))
    router = w_router is not None
    if router:
        args.append(w_router)
        in_specs.append(pl.BlockSpec((d, LANES), lambda i: (0, 0)))
        out_shape.append(jax.ShapeDtypeStruct((n, LANES), F32))
        out_specs.append(pl.BlockSpec((tm, LANES), lambda i: (i, 0)))
    return pl.pallas_call(
        functools.partial(_resid_norm_kernel, h_mode=h_mode, router=router,
                          tiles_per_seq=None if seq is None else seq // tm),
        grid=(n // tm,),
        in_specs=in_specs, out_specs=out_specs, out_shape=out_shape,
        scratch_shapes=scratch,
        compiler_params=_cparams(1),
        name="resid_norm",
    )(*args)


N_SCHED = 9
KIND_NONE, KIND_COMPUTE, KIND_ZERO, KIND_HALF = 0, 1, 2, 3


def _gmm_kernel(tile_ref, arow_ref, kind_ref, blk_ref, pfg_ref, pfw_ref, pfc_ref, pfd_ref,
                nblk_ref, *refs, n_a, n_w, n_col, has_scale, transposed, n_valid, merge_cast,
                half_tiles):
    del tile_ref, arow_ref, pfg_ref
    a_refs = refs[:n_a]
    c_refs = refs[n_a:n_a + n_w]
    pos = n_a + n_w
    s_ref = None
    if has_scale:
        s_ref = refs[pos]; pos += 1
    o_ref = refs[pos]; pos += 1
    slots = (refs[pos:pos + n_w], refs[pos + n_w:pos + 2 * n_w])
    jj = pl.program_id(0)
    s = pl.program_id(1)
    parity = (jj * nblk_ref[0] + blk_ref[s] + nblk_ref[0]) % 2
    rows_per_chunk = c_refs[0].shape[1]
    tcol = jj - 1 + pfw_ref[s]
    chunk_due = (pfd_ref[s] == 1) & (tcol >= 0) & (tcol < n_col)
    kind = kind_ref[s]
    computing = ((kind == KIND_COMPUTE) | (kind == KIND_HALF)) & (jj >= 1)
    tm = o_ref.shape[0]

    def cast_chunk(nxt):
        r = pl.multiple_of(pfc_ref[s] * rows_per_chunk, rows_per_chunk)
        for c_ref, wbf in zip(c_refs, nxt):
            wbf[pl.ds(r, rows_per_chunk), :] = c_ref[0].astype(BF16)

    def matmul(wbf, rows):
        if transposed:
            return lax.dot_general(a_refs[0][0:rows, :], wbf[...], (((1,), (1,)), ((), ())),
                                   preferred_element_type=F32)
        acc = None
        off = 0
        for a_ref in a_refs:
            ka = a_ref.shape[1]
            part = jnp.dot(a_ref[0:rows, :], wbf[off:off + ka, :], preferred_element_type=F32)
            acc = part if acc is None else acc + part
            off += ka
        return acc

    def compute(cur, nxt, rows):
        if merge_cast:
            cast_chunk(nxt)
        acc = matmul(cur[0], rows)
        if n_w == 2:
            acc = (acc * _sigmoid(acc)) * matmul(cur[1], rows)
        if has_scale:
            acc = acc * s_ref[...]
        if n_valid is not None:
            col = lax.broadcasted_iota(I32, acc.shape, 1) + (jj - 1) * acc.shape[1]
            acc = jnp.where(col < n_valid, acc, 0.0)
        o_ref[0:rows, :] = acc.astype(o_ref.dtype)
        if rows < tm:
            o_ref[rows:tm, :] = jnp.zeros((tm - rows, o_ref.shape[1]), o_ref.dtype)

    heights = ((KIND_COMPUTE, tm), (KIND_HALF, tm // 2)) if half_tiles else ((KIND_COMPUTE, tm),)
    for p in (0, 1):
        cur, nxt = slots[p], slots[1 - p]
        for which, rows in heights:
            pl.when((kind == which) & (jj >= 1) & (parity == p))(
                functools.partial(compute, cur, nxt, rows))
        cast_here = chunk_due & jnp.logical_not(computing) if merge_cast else chunk_due
        pl.when(cast_here & (parity == p))(functools.partial(cast_chunk, nxt))

    @pl.when((kind_ref[s] == KIND_ZERO) & (jj >= 1))
    def _():
        o_ref[...] = jnp.zeros_like(o_ref)


def _dense_schedule(n_tiles, nc):
    t = jnp.arange(n_tiles, dtype=I32)
    zeros = jnp.zeros((n_tiles,), I32)
    ones = jnp.ones((n_tiles,), I32)
    return (t, t, ones, zeros, zeros, ones, jnp.minimum(t, nc - 1), (t < nc).astype(I32),
            jnp.array([1, 0], I32))


def gmm(a, ws, *, tm, tn, n_out, out_dtype, nc, sched=None, k=None, scale=None,
        a_col=None, w_group=None, w_col=None, transposed=False, n_valid=None,
        merge_cast=True, n_steps=None, half_tiles=False):
    a_parts = list(a) if isinstance(a, (list, tuple)) else [a]
    rows = a_parts[0].shape[0]
    if k is None:
        k = sum(p.shape[1] for p in a_parts)
    if sched is None:
        sched = _dense_schedule(rows // tm, nc)
    if n_steps is None:
        n_steps = sched[0].shape[0]
    n_col = pl.cdiv(n_out, tn)
    a_col = a_col or (lambda j: 0)
    w_col = w_col or (lambda j: j)

    def cur_col(jj):
        return jnp.maximum(jj - 1, 0)

    def a_map(jj, s, tile, arow, *_):
        return (jnp.where(jj == 0, arow[0], arow[s]), a_col(cur_col(jj)))

    def o_map(jj, s, tile, *_):
        return (jnp.where(jj == 0, tile[0], tile[s]), cur_col(jj))

    def c_map(jj, s, tile, arow, kind, blk, pfg, pfw, pfc, pfd, meta):
        tcol = jj - 1 + pfw[s]
        col = jnp.clip(tcol, 0, n_col - 1)
        idle = tcol < 0
        chunk = jnp.where(idle, 0, pfc[s])
        group = jnp.where(idle, meta[1], pfg[s]) if w_group is None else w_group(col)
        if transposed:
            return (group, w_col(col) * nc + chunk, 0)
        return (group, chunk, w_col(col))

    in_specs = [pl.BlockSpec((tm, k if len(a_parts) == 1 else p.shape[1]), a_map)
                for p in a_parts]
    if transposed:
        assert len(a_parts) == 1 and tn % nc == 0
        chunk, wbf_shape = (1, tn // nc, k), (tn, k)
    else:
        assert k % nc == 0
        chunk, wbf_shape = (1, k // nc, tn), (k, tn)
    in_specs += [pl.BlockSpec(chunk, c_map) for _ in ws]
    args = [*a_parts, *ws]
    if scale is not None:
        in_specs.append(pl.BlockSpec((1, tn), lambda jj, s, *_: (0, cur_col(jj))))
        args.append(scale)
    grid_spec = pltpu.PrefetchScalarGridSpec(
        num_scalar_prefetch=N_SCHED,
        grid=(n_col + 1, n_steps),
        in_specs=in_specs,
        out_specs=pl.BlockSpec((tm, tn), o_map),
        scratch_shapes=[pltpu.VMEM(wbf_shape, BF16) for _ in range(2 * len(ws))],
    )
    return pl.pallas_call(
        functools.partial(_gmm_kernel, n_a=len(a_parts), n_w=len(ws), n_col=n_col,
                          has_scale=scale is not None, transposed=transposed,
                          n_valid=n_valid, merge_cast=merge_cast, half_tiles=half_tiles),
        grid_spec=grid_spec,
        out_shape=jax.ShapeDtypeStruct((rows, n_out), out_dtype),
        compiler_params=_cparams(2),
        name="gmm",
    )(*sched, *args)


def _forget_cumsum_kernel(f_ref, b_ref, o_ref):
    x = f_ref[...] + b_ref[...]
    ls = jnp.minimum(x, 0.0) - jnp.log1p(jnp.exp(-jnp.abs(x)))
    s = ls.shape[0]
    row = lax.broadcasted_iota(I32, ls.shape, 0)
    sh = 1
    while sh < s:
        ls = ls + jnp.where(row >= sh, pltpu.roll(ls, sh, 0), 0.0)
        sh *= 2
    o_ref[...] = ls


def forget_cumsum(f, b, batch):
    n, w = f.shape
    s = n // batch
    return pl.pallas_call(
        _forget_cumsum_kernel,
        grid=(batch,),
        in_specs=[pl.BlockSpec((s, w), lambda i: (i, 0)),
                  pl.BlockSpec((1, w), lambda i: (0, 0))],
        out_specs=pl.BlockSpec((s, w), lambda i: (i, 0)),
        out_shape=jax.ShapeDtypeStruct((n, w), F32),
        compiler_params=_cparams(1),
        name="forget_cumsum",
    )(f, b)


def _retention_kernel(lg_ref, q_ref, k_ref, v_ref, g_ref, cos_ref, sin_ref, o_ref):
    c = RET_CHUNK
    d = HEAD_DIM
    s = q_ref.shape[0]
    lg = lg_ref[pl.program_id(1)]
    ii = lax.broadcasted_iota(I32, (c, c), 0)
    jj = lax.broadcasted_iota(I32, (c, c), 1)
    diff = (ii - jj).astype(F32)
    decay_in = jnp.where(diff >= 0, jnp.exp(lg * jnp.maximum(diff, 0.0)), 0.0)
    jc = lax.broadcasted_iota(I32, (c, 1), 0).astype(F32)
    zeta = jnp.exp(lg * (c - 1 - jc))
    q_decay = jnp.exp(lg * (jc + 1.0))
    gamma_c = jnp.exp(jnp.full((1, 1), lg, F32) * c)
    scale = d ** -0.5
    nt = (((1,), (1,)), ((), ()))
    tn = (((0,), (0,)), ((), ()))
    state = jnp.zeros((d, d), F32)
    for n in range(s // c):
        sl = pl.ds(n * c, c)
        cos = cos_ref[sl, :]
        sin = sin_ref[sl, :]
        q = q_ref[sl, :].astype(F32)
        k = k_ref[sl, :].astype(F32)
        v = v_ref[sl, :]
        qr = q * cos + pltpu.roll(q, d // 2, 1) * sin
        kr = (k * cos + pltpu.roll(k, d // 2, 1) * sin) * scale
        scores = lax.dot_general(qr.astype(BF16), kr.astype(BF16), nt,
                                 preferred_element_type=F32) * decay_in
        inner = jnp.dot(scores.astype(BF16), v, preferred_element_type=F32)
        cross = jnp.dot((qr * q_decay).astype(BF16), state.astype(BF16),
                        preferred_element_type=F32)
        kv = lax.dot_general((kr * zeta).astype(BF16), v, tn, preferred_element_type=F32)
        state = gamma_c * state + kv
        o = inner + cross
        xc = o - jnp.mean(o, axis=-1, keepdims=True)
        y = xc * lax.rsqrt(jnp.mean(xc * xc, axis=-1, keepdims=True) + EPS)
        g = g_ref[sl, :].astype(F32)
        o_ref[sl, :] = ((g * _sigmoid(g)) * y).astype(o_ref.dtype)


def retention(proj, cos, sin, log_gamma, batch):
    n = proj.shape[0]
    s = n // batch
    d = HEAD_DIM
    hh = N_RET_HEADS

    def col(off):
        return pl.BlockSpec((s, d), lambda b, h, lg: (b, off + h))

    tab = pl.BlockSpec((s, d), lambda b, h, lg: (0, 0))
    grid_spec = pltpu.PrefetchScalarGridSpec(
        num_scalar_prefetch=1,
        grid=(batch, hh),
        in_specs=[col(0), col(hh), col(2 * hh), col(3 * hh), tab, tab],
        out_specs=pl.BlockSpec((s, d), lambda b, h, lg: (b, h)),
    )
    return pl.pallas_call(
        _retention_kernel,
        grid_spec=grid_spec,
        out_shape=jax.ShapeDtypeStruct((n, hh * d), BF16),
        compiler_params=_cparams(2),
        name="retention",
    )(log_gamma, proj, proj, proj, proj, cos, sin)


def _fox_kernel(q_ref, k_ref, v_ref, ccol_ref, crow_ref, o_ref, *, tq):
    s, d = q_ref.shape
    h = pl.program_id(1)
    scale = d ** -0.5
    nt = (((1,), (1,)), ((), ()))
    lane = lax.broadcasted_iota(I32, (tq, LANES), 1)
    rr = lax.broadcasted_iota(I32, (tq, tq), 0)
    cc = lax.broadcasted_iota(I32, (tq, tq), 1)

    for qi in range(s // tq):
        qs = pl.ds(qi * tq, tq)
        q = q_ref[qs, :]
        cq = jnp.sum(jnp.where(lane == h, ccol_ref[qs, :], 0.0), axis=-1, keepdims=True)

        def scores(kb):
            ks = pl.ds(pl.multiple_of(kb * tq, tq), tq)
            sc = lax.dot_general(q, k_ref[ks, :], nt, preferred_element_type=F32) * scale
            return sc + cq - crow_ref[:, ks], v_ref[ks, :]

        def update(carry, sc, v):
            m, l, acc = carry
            m_new = jnp.maximum(m, jnp.max(sc, axis=-1, keepdims=True))
            alpha = jnp.exp(m - m_new)
            p = jnp.exp(sc - m_new)
            l = alpha * l + jnp.sum(p, axis=-1, keepdims=True)
            acc = alpha * acc + jnp.dot(p.astype(BF16), v, preferred_element_type=F32)
            return m_new, l, acc

        def body(kb, carry):
            sc, v = scores(kb)
            return update(carry, sc, v)

        carry = (jnp.full((tq, 1), NEG_BIG, F32), jnp.zeros((tq, 1), F32),
                 jnp.zeros((tq, d), F32))
        carry = lax.fori_loop(0, qi, body, carry, unroll=True)
        sc, v = scores(qi)
        _, l, acc = update(carry, jnp.where(rr >= cc, sc, NEG_BIG), v)
        o_ref[qs, :] = (acc / l).astype(o_ref.dtype)


def fox_attention(proj, c_cols, c_rows, batch, tq=256):
    n = proj.shape[0]
    s = n // batch
    d = HEAD_DIM
    base = 4 * N_RET_HEADS
    hh = N_FOX_HEADS

    def col(off):
        return pl.BlockSpec((s, d), lambda b, h: (b, off + h))

    return pl.pallas_call(
        functools.partial(_fox_kernel, tq=tq),
        grid=(batch, hh),
        in_specs=[col(base), col(base + hh), col(base + 2 * hh),
                  pl.BlockSpec((s, LANES), lambda b, h: (b, 0)),
                  pl.BlockSpec((None, None, 1, s), lambda b, h: (b, h, 0, 0))],
        out_specs=pl.BlockSpec((s, d), lambda b, h: (b, h)),
        out_shape=jax.ShapeDtypeStruct((n, hh * d), BF16),
        compiler_params=_cparams(2),
        name="fox_attention",
    )(proj, proj, proj, c_cols, c_rows)


def _route_kernel(lg_ref, ids_ref, gate_ref, cnt_ref, carry_ref):
    tt = lg_ref.shape[0]

    @pl.when(pl.program_id(0) == 0)
    def _():
        carry_ref[...] = jnp.zeros_like(carry_ref)

    lane = lax.broadcasted_iota(I32, (tt, LANES), 1)
    lane_f = lane.astype(F32)
    x = jnp.where(lane < N_EXPERTS, lg_ref[...], -jnp.inf)
    v1 = jnp.max(x, axis=-1, keepdims=True)
    i1 = jnp.min(jnp.where(x == v1, lane_f, float(LANES)), axis=-1, keepdims=True)
    x2 = jnp.where(lane_f == i1, -jnp.inf, x)
    v2 = jnp.max(x2, axis=-1, keepdims=True)
    i2 = jnp.min(jnp.where(x2 == v2, lane_f, float(LANES)), axis=-1, keepdims=True)
    e = jnp.exp(v2 - v1)
    w1 = 1.0 / (1.0 + e)
    w2 = e / (1.0 + e)
    sel1 = lane_f == i1
    sel2 = lane_f == i2
    onehot = jnp.where(sel1, 1.0, jnp.where(sel2, 1.0, 0.0))
    rr = lax.broadcasted_iota(I32, (tt, tt), 0)
    cc = lax.broadcasted_iota(I32, (tt, tt), 1)
    tri = jnp.where(rr > cc, 1.0, 0.0).astype(BF16)
    before = jnp.dot(tri, onehot.astype(BF16), preferred_element_type=F32) + carry_ref[0:1, :]
    r1 = jnp.sum(jnp.where(sel1, before, 0.0), axis=-1, keepdims=True)
    r2 = jnp.sum(jnp.where(sel2, before, 0.0), axis=-1, keepdims=True)
    ids = jnp.where(lane == 0, i1, jnp.where(lane == 1, i2,
                    jnp.where(lane == 2, r1, jnp.where(lane == 3, r2, 0.0))))
    ids_ref[...] = ids.astype(I32)
    gate_ref[...] = jnp.where(lane == 0, w1, jnp.where(lane == 1, w2, 0.0))
    total = carry_ref[0:1, :] + jnp.sum(onehot, axis=0, keepdims=True)
    carry_ref[...] = jnp.broadcast_to(total, carry_ref.shape)
    cnt_ref[...] = jnp.broadcast_to(total, cnt_ref.shape).astype(I32)


def route(logits, tt=256):
    n = logits.shape[0]
    row = pl.BlockSpec((tt, LANES), lambda i: (i, 0))
    return pl.pallas_call(
        _route_kernel,
        grid=(n // tt,),
        in_specs=[row],
        out_specs=[row, row, pl.BlockSpec((8, LANES), lambda i: (0, 0))],
        out_shape=[jax.ShapeDtypeStruct((n, LANES), I32),
                   jax.ShapeDtypeStruct((n, LANES), F32),
                   jax.ShapeDtypeStruct((8, LANES), I32)],
        scratch_shapes=[pltpu.VMEM((8, LANES), F32)],
        compiler_params=_cparams(1),
        name="route",
    )(logits)


def _gather_kernel(pos1_ref, pos2_ref, h_hbm, o_ref, src_ref, buf_ref, sems, *,
                   n_tokens, nb):
    tg = o_ref.shape[0]
    i = pl.program_id(0)
    n_tiles = pl.num_programs(0) - 1

    @pl.when(i == 0)
    def _():
        def clear(p, carry):
            src_ref[p] = 0
            return carry
        lax.fori_loop(0, src_ref.shape[0], clear, 0, unroll=8)

        def place(t, carry):
            src_ref[pos1_ref[t]] = t
            src_ref[pos2_ref[t]] = t
            return carry
        lax.fori_loop(0, n_tokens, place, 0, unroll=4)

    def slab_copy(tile, slot, r):
        src = pl.multiple_of(src_ref[tile * tg + r] * nb, nb)
        dst = pl.multiple_of(r * nb, nb)
        return pltpu.make_async_copy(
            h_hbm.at[pl.ds(src, nb), :], buf_ref.at[slot, pl.ds(dst, nb), :], sems.at[slot])

    @pl.when(i < n_tiles)
    def _():
        def issue(pair, carry):
            for u in range(2):
                slab_copy(i, i % 2, 2 * pair + u).start(priority=u)
            return carry
        lax.fori_loop(0, tg // 2, issue, 0, unroll=8)

    @pl.when(i > 0)
    def _():
        slot = (i - 1) % 2

        def drain(r, carry):
            slab_copy(i - 1, slot, r).wait()
            return carry
        lax.fori_loop(0, tg, drain, 0, unroll=4)
        half = nb * LANES
        for c in range(nb):
            lo, hi = _unpack_bf16_pairs(buf_ref[slot, pl.ds(c, tg, stride=nb), :])
            o_ref[:, c * LANES:(c + 1) * LANES] = lo
            o_ref[:, half + c * LANES:half + (c + 1) * LANES] = hi


def gather_rows(h_packed, pos1, pos2, n_rows, d, tg=256):
    nb = d // 2 // LANES
    n = h_packed.shape[0] // nb
    width = 2 * nb * LANES
    n_tiles = n_rows // tg
    grid_spec = pltpu.PrefetchScalarGridSpec(
        num_scalar_prefetch=2,
        grid=(n_tiles + 1,),
        in_specs=[pl.BlockSpec(memory_space=pl.ANY)],
        out_specs=pl.BlockSpec((tg, width), lambda i, p1, p2: (jnp.maximum(i - 1, 0), 0)),
        scratch_shapes=[pltpu.SMEM((n_rows,), I32), pltpu.VMEM((2, tg * nb, LANES), U32),
                        pltpu.SemaphoreType.DMA((2,))],
    )
    return pl.pallas_call(
        functools.partial(_gather_kernel, n_tokens=n, nb=nb),
        grid_spec=grid_spec,
        out_shape=jax.ShapeDtypeStruct((n_rows, width), BF16),
        compiler_params=_cparams(1),
        name="gather_rows",
    )(pos1, pos2, h_packed)


def _combine_kernel(pos1_ref, pos2_ref, o_hbm, gate_ref, x_ref, g_ref, out_ref, buf_ref, sems):
    tt = x_ref.shape[0]
    i = pl.program_id(0)
    n_tiles = pl.num_programs(0) - 1

    def row_copy(tile, slot, r, which, pos_ref):
        return pltpu.make_async_copy(
            o_hbm.at[pl.ds(pos_ref[tile * tt + r], 1), :],
            buf_ref.at[slot, which, pl.ds(r, 1), :], sems.at[slot])

    @pl.when(i < n_tiles)
    def _():
        def issue(r, carry):
            row_copy(i, i % 2, r, 0, pos1_ref).start(priority=0)
            row_copy(i, i % 2, r, 1, pos2_ref).start(priority=1)
            return carry
        lax.fori_loop(0, tt, issue, 0, unroll=8)

    @pl.when(i > 0)
    def _():
        slot = (i - 1) % 2

        def drain(r, carry):
            row_copy(i - 1, slot, r, 0, pos1_ref).wait()
            row_copy(i - 1, slot, r, 1, pos2_ref).wait()
            return carry
        lax.fori_loop(0, tt, drain, 0, unroll=4)
        gate = gate_ref[...]
        y = gate[:, 0:1] * buf_ref[slot, 0] + gate[:, 1:2] * buf_ref[slot, 1]
        out_ref[...] = x_ref[...] + (y * _rms(y)) * g_ref[...]


def combine(o, pos1, pos2, gate, x, g, tt=128):
    n, d = x.shape
    row = lambda i, p1, p2: (jnp.maximum(i - 1, 0), 0)
    grid_spec = pltpu.PrefetchScalarGridSpec(
        num_scalar_prefetch=2,
        grid=(n // tt + 1,),
        in_specs=[pl.BlockSpec(memory_space=pl.ANY),
                  pl.BlockSpec((tt, LANES), row),
                  pl.BlockSpec((tt, d), row),
                  pl.BlockSpec((1, d), lambda i, p1, p2: (0, 0))],
        out_specs=pl.BlockSpec((tt, d), row),
        scratch_shapes=[pltpu.VMEM((2, 2, tt, d), F32), pltpu.SemaphoreType.DMA((2,))],
    )
    return pl.pallas_call(
        _combine_kernel,
        grid_spec=grid_spec,
        out_shape=jax.ShapeDtypeStruct((n, d), F32),
        compiler_params=_cparams(1),
        name="combine",
    )(pos1, pos2, o, gate, x, g.reshape(1, d))


def _rope_tables(s):
    half = HEAD_DIM // 2
    inv = ROPE_BASE ** (-jnp.arange(half, dtype=F32) / half)
    ang = jnp.arange(s, dtype=F32)[:, None] * inv[None, :]
    cos, sin = jnp.cos(ang), jnp.sin(ang)
    return jnp.concatenate([cos, cos], axis=-1), jnp.concatenate([-sin, sin], axis=-1)


def _moe_schedule(ids, counts, tm, n_tiles, nc):
    ne = N_EXPERTS
    counts = counts[0, :ne]
    tiles = (counts + tm - 1) // tm
    nonempty = tiles > 0
    tile_end = jnp.cumsum(tiles)
    tile_start = tile_end - tiles
    pos1 = tile_start[ids[:, 0]] * tm + ids[:, 2]
    pos2 = tile_start[ids[:, 1]] * tm + ids[:, 3]

    steps = jnp.where(nonempty, jnp.maximum(tiles, nc), 0)
    step_end = jnp.cumsum(steps)
    step_start = step_end - steps
    total, n_used = step_end[-1], tile_end[-1]
    ar = jnp.arange(ne, dtype=I32)
    first_e = jnp.min(jnp.where(nonempty, ar, ne))
    last_e = jnp.max(jnp.where(nonempty, ar, -1))
    later = (ar[None, :] > ar[:, None]) & nonempty[None, :]
    nxt = jnp.min(jnp.where(later, ar[None, :], ne), axis=1)
    wrap = nxt == ne
    nxt = jnp.where(wrap, first_e, nxt)
    rank = jnp.cumsum(nonempty.astype(I32)) - 1

    s = jnp.arange(n_tiles + ne * (nc - 1), dtype=I32)
    in_blocks = s < total
    e = jnp.minimum(jnp.sum((s[:, None] >= step_end[None, :]).astype(I32), axis=1), ne - 1)
    e = jnp.where(in_blocks, e, last_e)
    p = s - step_start[e]
    is_comp = in_blocks & (p < tiles[e])
    z = s - total
    is_zero = (~in_blocks) & (z < n_tiles - n_used)
    group_last = tile_start[e] + tiles[e] - 1
    tile = jnp.where(is_comp, tile_start[e] + p,
                     jnp.where(is_zero, n_used + z, jnp.where(in_blocks, group_last, n_tiles - 1)))
    arow = jnp.where(is_comp, tile, jnp.where(in_blocks, group_last, n_used - 1))
    tail_rows = counts[e] - (tiles[e] - 1) * tm
    is_half = is_comp & (p == tiles[e] - 1) & (tail_rows <= tm // 2)
    kind = jnp.where(is_half, KIND_HALF,
                     jnp.where(is_comp, KIND_COMPUTE, jnp.where(is_zero, KIND_ZERO, KIND_NONE)))
    pf_chunk = jnp.where(in_blocks, jnp.minimum(p, nc - 1), nc - 1)
    pf_do = in_blocks & (p < nc)
    sched = (tile, arow, kind, rank[e], nxt[e], wrap[e], pf_chunk, pf_do,
             jnp.stack([jnp.sum(nonempty.astype(I32)), first_e]))
    n_steps = (total + n_tiles - n_used).astype(I32)
    return pos1.astype(I32), pos2.astype(I32), tuple(v.astype(I32) for v in sched), n_steps


def kernel(x, even_norm_mix_pre, even_w_in, even_b_forget, even_w_out, even_norm_mix_post,
           even_norm_ffn_pre, even_w_gate, even_w_up, even_w_down, even_norm_ffn_post,
           odd_norm_mix_pre, odd_w_pool, odd_pool_scale, odd_norm_mix_post, odd_norm_ffn_pre,
           odd_w_router, odd_we_gate, odd_we_up, odd_we_down, odd_norm_ffn_post):
    batch, seq, d = x.shape
    n = batch * seq
    x0 = x.reshape(n, d)
    ret_w = N_RET_HEADS * HEAD_DIM
    fox_w = N_FOX_HEADS * HEAD_DIM
    main_cols = 4 * ret_w + 3 * fox_w

    h0 = rms_cast(x0, even_norm_mix_pre[0])
    w_in_t = jnp.swapaxes(even_w_in, 1, 2)
    proj = gmm(h0, [w_in_t], tm=1024, tn=1024, n_out=main_cols, out_dtype=BF16, nc=4,
               transposed=True)
    b_forget = jnp.pad(even_b_forget[0], (0, LANES - N_FOX_HEADS)).reshape(1, LANES)
    f_logit = gmm(h0, [w_in_t], tm=512, tn=LANES, n_out=LANES, out_dtype=F32, nc=1,
                  transposed=True, w_col=lambda j: main_cols // LANES + j,
                  n_valid=N_FOX_HEADS)
    c_cols = forget_cumsum(f_logit, b_forget, batch)
    c_rows = c_cols[:, :N_FOX_HEADS].reshape(batch, seq, N_FOX_HEADS)
    c_rows = c_rows.transpose(0, 2, 1).reshape(batch, N_FOX_HEADS, 1, seq)
    cos, sin = _rope_tables(seq)
    log_gamma = jnp.log1p(-(2.0 ** (-5.0 - jnp.arange(N_RET_HEADS, dtype=F32))))
    ret = retention(proj, cos, sin, log_gamma, batch)
    fox = fox_attention(proj, c_cols, c_rows, batch)
    m = gmm([ret, fox], [even_w_out], tm=1024, tn=1024, n_out=d, out_dtype=F32, nc=8)
    x1, h1 = resid_norm(x0, m, even_norm_mix_post[0], even_norm_ffn_pre[0], "bf16")

    d_ff = even_w_gate.shape[-1]
    act = gmm(h1, [even_w_gate, even_w_up], tm=1024, tn=512, n_out=d_ff, out_dtype=BF16, nc=4)
    f = gmm(act, [even_w_down], tm=512, tn=512, n_out=d, out_dtype=F32, nc=16)
    x2, pooled = resid_norm(x1, f, even_norm_ffn_post[0], odd_norm_mix_pre[0], "pooled",
                            seq=seq)
    group = d // len(POOL_WINDOWS)
    tn_pool = 512
    per = group // tn_pool
    m2 = gmm(pooled, [odd_w_pool[0]], tm=2048, tn=tn_pool, n_out=d, out_dtype=F32, nc=2,
             k=group, scale=odd_pool_scale[0].reshape(1, d),
             a_col=lambda j: j // per, w_group=lambda j: j // per, w_col=lambda j: j % per)
    w_router = jnp.pad(odd_w_router[0], ((0, 0), (0, LANES - N_EXPERTS)))
    x3, h3, logits = resid_norm(x2, m2, odd_norm_mix_post[0], odd_norm_ffn_pre[0], "packed",
                                w_router=w_router)

    ids, gate, counts = route(logits)
    tm = MOE_TILE
    n_rows = 2 * n + N_EXPERTS * tm
    pos1, pos2, sched, n_steps = _moe_schedule(ids, counts, tm, n_rows // tm, MOE_CHUNKS)
    xs = gather_rows(h3, pos1, pos2, n_rows, d)
    d_fe = odd_we_gate.shape[-1]
    act2 = gmm(xs, [odd_we_gate[0], odd_we_up[0]], tm=tm, tn=1024, n_out=d_fe,
               out_dtype=BF16, nc=MOE_CHUNKS, sched=sched, merge_cast=False, n_steps=n_steps,
               half_tiles=True)
    o = gmm(act2, [odd_we_down[0]], tm=tm, tn=1024, n_out=d, out_dtype=F32,
            nc=MOE_CHUNKS, sched=sched, merge_cast=False, n_steps=n_steps, half_tiles=True)
    out = combine(o, pos1, pos2, gate, x3, odd_norm_ffn_post[0])
    return out.reshape(batch, seq, d)
```
